```python
import jax, jax.numpy as jnp
from jax import lax
import numpy as np

D_MODEL = 2048
BATCH = 4
SEQ = 4096
DEPTH = 4

N_MEM = 256
GRID_W = 64
HEAD_DIM = 128
MIX_WIDTH = D_MODEL
MEM_HEADS = 4
MEM_WIDTH = MEM_HEADS * HEAD_DIM
TOK_WIDTH = MIX_WIDTH - MEM_WIDTH
CHUNK = 128
A_GROUPS = TOK_WIDTH // HEAD_DIM
A_GROUP_DIM = HEAD_DIM
Q_HEADS = TOK_WIDTH // HEAD_DIM
KV_HEADS = 4
Q_PER_KV = Q_HEADS // KV_HEADS
KV_WIDTH = KV_HEADS * HEAD_DIM
Q_BLOCK = 128
ROPE_THETA = 10000.0
ROPE_PAIRS = HEAD_DIM // 4
D_FF = ((8 * D_MODEL // 3 + 255) // 256) * 256
N_MIXERS = 2
N_A = (DEPTH + 1) // 2
N_B = DEPTH // 2
EPS = 1e-6

kernel_name = "hybrid_gmlp_axial_gqa_memory_encoder"


def rms_norm(x, g):
    xf = x.astype(jnp.float32)
    y = xf * lax.rsqrt(jnp.mean(xf * xf, axis=-1, keepdims=True) + EPS)
    return (y * g.astype(jnp.float32)).astype(x.dtype)


def axial_rope_tables(seq_len):
    n_rows = seq_len // GRID_W
    rows = jnp.broadcast_to(jnp.arange(n_rows)[:, None], (n_rows, GRID_W)).reshape(seq_len)
    cols = jnp.broadcast_to(jnp.arange(GRID_W)[None, :], (n_rows, GRID_W)).reshape(seq_len)
    freqs = ROPE_THETA ** (-jnp.arange(ROPE_PAIRS, dtype=jnp.float32) / ROPE_PAIRS)
    ang_r = rows.astype(jnp.float32)[:, None] * freqs
    ang_c = cols.astype(jnp.float32)[:, None] * freqs
    ang = jnp.concatenate([ang_r, ang_r, ang_c, ang_c], axis=-1)
    return jnp.cos(ang), jnp.sin(ang)


def apply_axial_rope(x, cos, sin):
    shape = (1, cos.shape[0]) + (1,) * (x.ndim - 3) + (HEAD_DIM,)
    c, s = cos.reshape(shape), sin.reshape(shape)
    xf = x.astype(jnp.float32)
    xs = xf.reshape(x.shape[:-1] + (2, 2, ROPE_PAIRS))
    rot = jnp.stack([-xs[..., 1, :], xs[..., 0, :]], axis=-2).reshape(x.shape)
    return (xf * c + rot * s).astype(x.dtype)


def chunked_spatial_gating(uv, g_v, w_s, b_s):
    B, S, _ = uv.shape
    uv = jax.nn.gelu(uv, approximate=False)
    u, v = jnp.split(uv, 2, axis=-1)
    v = rms_norm(v, g_v)
    v = v.reshape(B, S // CHUNK, CHUNK, A_GROUPS, A_GROUP_DIM)
    s = jnp.einsum('gts,bcsgd->bctgd', w_s, v) + b_s.T[None, None, :, :, None]
    return u * s.reshape(B, S, TOK_WIDTH)


def gqa_axial_attention(qkv, g_q, g_k, cos, sin):
    B, S, _ = qkv.shape
    q, k, v = jnp.split(qkv, [TOK_WIDTH, TOK_WIDTH + KV_WIDTH], axis=-1)
    q = q.reshape(B, S, KV_HEADS, Q_PER_KV, HEAD_DIM)
    k = k.reshape(B, S, KV_HEADS, HEAD_DIM)
    v = v.reshape(B, S, KV_HEADS, HEAD_DIM)
    q = apply_axial_rope(rms_norm(q, g_q), cos, sin)
    k = apply_axial_rope(rms_norm(k, g_k), cos, sin)
    scale = HEAD_DIM ** -0.5
    n_blk = S // Q_BLOCK
    qb = q.reshape(B, n_blk, Q_BLOCK, KV_HEADS, Q_PER_KV, HEAD_DIM).transpose(1, 0, 2, 3, 4, 5)

    def block(qi):
        s = jnp.einsum('bqhgd,bkhd->bhgqk', qi, k).astype(jnp.float32) * scale
        p = jax.nn.softmax(s, axis=-1).astype(v.dtype)
        return jnp.einsum('bhgqk,bkhd->bqhgd', p, v)

    o = lax.map(block, qb)
    return o.transpose(1, 0, 2, 3, 4, 5).reshape(B, S, TOK_WIDTH)


def memory_attention(q_mem, mem, g_mem, w_kv, g_mq, g_mk):
    B, S, _ = q_mem.shape
    kv = rms_norm(mem, g_mem) @ w_kv
    k, v = jnp.split(kv, 2, axis=-1)
    k = rms_norm(k.reshape(B, -1, MEM_HEADS, HEAD_DIM), g_mk)
    v = v.reshape(B, -1, MEM_HEADS, HEAD_DIM)
    q = rms_norm(q_mem.reshape(B, S, MEM_HEADS, HEAD_DIM), g_mq)
    s = jnp.einsum('bshd,bmhd->bhsm', q, k).astype(jnp.float32) * (HEAD_DIM ** -0.5)
    p = jax.nn.softmax(s, axis=-1).astype(v.dtype)
    return jnp.einsum('bhsm,bmhd->bshd', p, v).reshape(B, S, MEM_WIDTH)


def swiglu(h, w_gate_up, w_down):
    gate, up = jnp.split(h @ w_gate_up, 2, axis=-1)
    return (jax.nn.silu(gate) * up) @ w_down


def setup_inputs(seed: int = 0) -> dict:
    key = jax.random.key(seed)
    ks = jax.random.split(key, 20)
    f32 = jnp.float32

    def nrm(k, shape, fan_in):
        return jax.random.normal(k, shape, f32) * (fan_in ** -0.5)

    def gain(k, shape):
        return 1.0 + 0.02 * jax.random.normal(k, shape, f32)

    return {
        "x": jax.random.normal(ks[0], (BATCH, SEQ, D_MODEL), f32),
        "mem": jax.random.normal(ks[1], (BATCH, N_MEM, D_MODEL), f32),
        "g_mix": gain(ks[2], (DEPTH, D_MODEL)),
        "g_ffn": gain(ks[3], (DEPTH, D_MODEL)),
        "w_in_a": nrm(ks[4], (N_A, D_MODEL, 2 * TOK_WIDTH + MEM_WIDTH), D_MODEL),
        "g_v_a": gain(ks[5], (N_A, TOK_WIDTH)),
        "w_spatial": nrm(ks[6], (N_A, A_GROUPS, CHUNK, CHUNK), CHUNK),
        "b_spatial": 0.02 * jax.random.normal(ks[7], (N_A, A_GROUPS, CHUNK), f32),
        "w_in_b": nrm(ks[8], (N_B, D_MODEL, TOK_WIDTH + 2 * KV_WIDTH + MEM_WIDTH), D_MODEL),
        "g_q_b": gain(ks[9], (N_B, HEAD_DIM)),
        "g_k_b": gain(ks[10], (N_B, HEAD_DIM)),
        "g_mem": gain(ks[11], (DEPTH, D_MODEL)),
        "w_mem_kv": nrm(ks[12], (DEPTH, D_MODEL, 2 * MEM_WIDTH), D_MODEL),
        "g_mq": gain(ks[13], (DEPTH, HEAD_DIM)),
        "g_mk": gain(ks[14], (DEPTH, HEAD_DIM)),
        "w_out": nrm(ks[15], (DEPTH, MIX_WIDTH, D_MODEL), MIX_WIDTH),
        "w_gate_up": nrm(ks[16], (DEPTH, D_MODEL, 2 * D_FF), D_MODEL),
        "w_down": nrm(ks[17], (DEPTH, D_FF, D_MODEL), D_FF),
    }


def reference(x, mem, g_mix, g_ffn, w_in_a, g_v_a, w_spatial, b_spatial, w_in_b, g_q_b, g_k_b,
              g_mem, w_mem_kv, g_mq, g_mk, w_out, w_gate_up, w_down):
    S = x.shape[1]
    cos, sin = axial_rope_tables(S)
    for l in range(DEPTH):
        h = rms_norm(x, g_mix[l])
        if l % N_MIXERS == 0:
            ia = l // N_MIXERS
            z = h @ w_in_a[ia]
            tok_in, q_mem = jnp.split(z, [2 * TOK_WIDTH], axis=-1)
            tok_out = chunked_spatial_gating(tok_in, g_v_a[ia], w_spatial[ia], b_spatial[ia])
        else:
            ib = l // N_MIXERS
            z = h @ w_in_b[ib]
            tok_in, q_mem = jnp.split(z, [TOK_WIDTH + 2 * KV_WIDTH], axis=-1)
            tok_out = gqa_axial_attention(tok_in, g_q_b[ib], g_k_b[ib], cos, sin)
        mem_out = memory_attention(q_mem, mem, g_mem[l], w_mem_kv[l], g_mq[l], g_mk[l])
        x = x + jnp.concatenate([tok_out, mem_out], axis=-1) @ w_out[l]
        x = x + swiglu(rms_norm(x, g_ffn[l]), w_gate_up[l], w_down[l])
    return x
```

```python
import functools

import jax
import jax.numpy as jnp
from jax import lax
from jax.experimental import pallas as pl
from jax.experimental.pallas import tpu as pltpu

D_MODEL = 2048
DEPTH = 4
N_MEM = 256
GRID_W = 64
HEAD_DIM = 128
MEM_HEADS = 4
MEM_WIDTH = MEM_HEADS * HEAD_DIM
TOK_WIDTH = D_MODEL - MEM_WIDTH
CHUNK = 128
A_GROUPS = TOK_WIDTH // HEAD_DIM
Q_HEADS = TOK_WIDTH // HEAD_DIM
KV_HEADS = 4
Q_PER_KV = Q_HEADS // KV_HEADS
KV_WIDTH = KV_HEADS * HEAD_DIM
ROPE_THETA = 10000.0
ROPE_PAIRS = HEAD_DIM // 4
EPS = 1e-6
SCORE_SCALE = HEAD_DIM ** -0.5

VMEM_LIMIT_BYTES = 56 * 1024 * 1024

IN_PROJ_ROWS = 1024
IN_PROJ_COLS = 512
MIX_ROWS = 256
ATTN_Q_ROWS = 256
FFN_ROWS = 512
FFN_COLS = 512

BF16 = jnp.bfloat16
F32 = jnp.float32


def _rms_scale(x):
    return lax.rsqrt(jnp.mean(x * x, axis=-1, keepdims=True) + EPS)


def _gelu_exact(x):
    return 0.5 * x * (1.0 + lax.erf(x * (2.0 ** -0.5)))


def _dot(a, b):
    return jnp.dot(a, b, preferred_element_type=F32)


def _params(*semantics):
    return pltpu.CompilerParams(dimension_semantics=semantics, vmem_limit_bytes=VMEM_LIMIT_BYTES)


def _mem_kv_kernel(mem_ref, g_mem_ref, w_ref, g_mk_ref, kt_ref, v_ref):
    m = mem_ref[...]
    h = (m * _rms_scale(m) * g_mem_ref[...]).astype(BF16)
    kv = _dot(h, w_ref[...])
    for hd in range(MEM_HEADS):
        k = kv[:, hd * HEAD_DIM:(hd + 1) * HEAD_DIM]
        k = k * _rms_scale(k) * g_mk_ref[...]
        kt_ref[hd * HEAD_DIM:(hd + 1) * HEAD_DIM, :] = k.T.astype(BF16)
    v_ref[...] = kv[:, MEM_WIDTH:].astype(BF16)


def _mem_kv(mem, g_mem, w_mem_kv, g_mk):
    batch = mem.shape[0]
    return pl.pallas_call(
        _mem_kv_kernel,
        grid=(DEPTH, batch),
        in_specs=[
            pl.BlockSpec((None, N_MEM, D_MODEL), lambda l, b: (b, 0, 0)),
            pl.BlockSpec((None, 1, D_MODEL), lambda l, b: (l, 0, 0)),
            pl.BlockSpec((None, D_MODEL, 2 * MEM_WIDTH), lambda l, b: (l, 0, 0)),
            pl.BlockSpec((None, 1, HEAD_DIM), lambda l, b: (l, 0, 0)),
        ],
        out_specs=[
            pl.BlockSpec((None, None, MEM_WIDTH, N_MEM), lambda l, b: (l, b, 0, 0)),
            pl.BlockSpec((None, None, N_MEM, MEM_WIDTH), lambda l, b: (l, b, 0, 0)),
        ],
        out_shape=[
            jax.ShapeDtypeStruct((DEPTH, batch, MEM_WIDTH, N_MEM), BF16),
            jax.ShapeDtypeStruct((DEPTH, batch, N_MEM, MEM_WIDTH), BF16),
        ],
        compiler_params=_params("arbitrary", "arbitrary"),
        name="mem_kv",
    )(mem, g_mem.reshape(DEPTH, 1, D_MODEL), w_mem_kv, g_mk.reshape(DEPTH, 1, HEAD_DIM))


def _norm_to_scratch(x_ref, g_ref, h_ref):
    x = x_ref[...]
    h_ref[...] = (x * _rms_scale(x) * g_ref[...]).astype(BF16)


def _in_proj_gmlp_kernel(x_ref, g_ref, w_ref, z_ref, h_ref, *, gelu_tiles):
    j = pl.program_id(1)

    @pl.when(j == 0)
    def _():
        _norm_to_scratch(x_ref, g_ref, h_ref)

    acc = _dot(h_ref[...], w_ref[...])

    @pl.when(j < gelu_tiles)
    def _():
        z_ref[...] = _gelu_exact(acc).astype(BF16)

    @pl.when(j >= gelu_tiles)
    def _():
        z_ref[...] = acc.astype(BF16)


def _rope_heads(acc, gain, cos, sin_lo, sin_hi, out_scale):
    outs = []
    for hd in range(acc.shape[1] // HEAD_DIM):
        a = acc[:, hd * HEAD_DIM:(hd + 1) * HEAD_DIM]
        a = a * _rms_scale(a) * gain
        r = a * cos + pltpu.roll(a, HEAD_DIM - ROPE_PAIRS, 1) * sin_lo + pltpu.roll(a, ROPE_PAIRS, 1) * sin_hi
        if out_scale is not None:
            r = r * out_scale
        outs.append(r.astype(BF16))
    return jnp.concatenate(outs, axis=1)


def _in_proj_attn_kernel(x_ref, g_ref, w_ref, gq_ref, gk_ref, cos_ref, slo_ref, shi_ref, z_ref, h_ref,
                         *, q_tiles, k_tiles):
    j = pl.program_id(1)

    @pl.when(j == 0)
    def _():
        _norm_to_scratch(x_ref, g_ref, h_ref)

    acc = _dot(h_ref[...], w_ref[...])

    @pl.when(j < q_tiles)
    def _():
        z_ref[...] = _rope_heads(acc, gq_ref[...], cos_ref[...], slo_ref[...], shi_ref[...], SCORE_SCALE)

    @pl.when(jnp.logical_and(j >= q_tiles, j < q_tiles + k_tiles))
    def _():
        z_ref[...] = _rope_heads(acc, gk_ref[...], cos_ref[...], slo_ref[...], shi_ref[...], None)

    @pl.when(j >= q_tiles + k_tiles)
    def _():
        z_ref[...] = acc.astype(BF16)


def _in_proj(x2d, g, w, *, attn=None):
    rows, _ = x2d.shape
    cols = w.shape[1]
    bm, bn = IN_PROJ_ROWS, IN_PROJ_COLS
    grid = (rows // bm, cols // bn)
    in_specs = [
        pl.BlockSpec((bm, D_MODEL), lambda i, j: (i, 0)),
        pl.BlockSpec((1, D_MODEL), lambda i, j: (0, 0)),
        pl.BlockSpec((D_MODEL, bn), lambda i, j: (0, j)),
    ]
    args = [x2d, g.reshape(1, D_MODEL), w]
    if attn is None:
        body = functools.partial(_in_proj_gmlp_kernel, gelu_tiles=2 * TOK_WIDTH // bn)
    else:
        g_q, g_k, cos, sin_lo, sin_hi = attn
        seq_tiles = cos.shape[0] // bm
        body = functools.partial(_in_proj_attn_kernel, q_tiles=TOK_WIDTH // bn, k_tiles=KV_WIDTH // bn)
        in_specs += [
            pl.BlockSpec((1, HEAD_DIM), lambda i, j: (0, 0)),
            pl.BlockSpec((1, HEAD_DIM), lambda i, j: (0, 0)),
            pl.BlockSpec((bm, HEAD_DIM), lambda i, j: (i % seq_tiles, 0)),
            pl.BlockSpec((bm, HEAD_DIM), lambda i, j: (i % seq_tiles, 0)),
            pl.BlockSpec((bm, HEAD_DIM), lambda i, j: (i % seq_tiles, 0)),
        ]
        args += [g_q.reshape(1, HEAD_DIM), g_k.reshape(1, HEAD_DIM), cos, sin_lo, sin_hi]
    return pl.pallas_call(
        body,
        grid=grid,
        in_specs=in_specs,
        out_specs=pl.BlockSpec((bm, bn), lambda i, j: (i, j)),
        out_shape=jax.ShapeDtypeStruct((rows, cols), BF16),
        scratch_shapes=[pltpu.VMEM((bm, D_MODEL), BF16)],
        compiler_params=_params("parallel", "arbitrary"),
        name="in_proj_gmlp" if attn is None else "in_proj_attn",
    )(*args)


def _attention_kernel(q_ref, k_ref, v_ref, o_ref):
    k = k_ref[...]
    v = v_ref[...]
    for g in range(Q_PER_KV):
        q = q_ref[:, g * HEAD_DIM:(g + 1) * HEAD_DIM]
        s = lax.dot_general(q, k, (((1,), (1,)), ((), ())), preferred_element_type=F32)
        p = jnp.exp(s - jnp.max(s, axis=-1, keepdims=True))
        denom = jnp.sum(p, axis=-1, keepdims=True)
        o = _dot(p.astype(BF16), v) / denom
        o_ref[:, g * HEAD_DIM:(g + 1) * HEAD_DIM] = o.astype(BF16)


def _attention(z, batch, seq):
    bq = ATTN_Q_ROWS
    q_tiles = seq // bq
    group_width = Q_PER_KV * HEAD_DIM
    k_col0 = TOK_WIDTH // HEAD_DIM
    v_col0 = (TOK_WIDTH + KV_WIDTH) // HEAD_DIM
    return pl.pallas_call(
        _attention_kernel,
        grid=(batch, KV_HEADS, q_tiles),
        in_specs=[
            pl.BlockSpec((bq, group_width), lambda b, h, i: (b * q_tiles + i, h)),
            pl.BlockSpec((seq, HEAD_DIM), lambda b, h, i: (b, k_col0 + h)),
            pl.BlockSpec((seq, HEAD_DIM), lambda b, h, i: (b, v_col0 + h)),
        ],
        out_specs=pl.BlockSpec((bq, group_width), lambda b, h, i: (b * q_tiles + i, h)),
        out_shape=jax.ShapeDtypeStruct((batch * seq, TOK_WIDTH), BF16),
        compiler_params=_params("parallel", "parallel", "arbitrary"),
        name="attention",
    )(z, z, z)


def _memory_attention_into(mixed_ref, qm_ref, g_mq_ref, kt_ref, v_ref):
    for hd in range(MEM_HEADS):
        lanes = slice(hd * HEAD_DIM, (hd + 1) * HEAD_DIM)
        q = qm_ref[:, lanes].astype(F32)
        q = (q * _rms_scale(q) * g_mq_ref[...] * SCORE_SCALE).astype(BF16)
        s = _dot(q, kt_ref[lanes, :])
        p = jnp.exp(s - jnp.max(s, axis=-1, keepdims=True))
        denom = jnp.sum(p, axis=-1, keepdims=True)
        o = _dot(p.astype(BF16), v_ref[:, lanes]) / denom
        mixed_ref[:, TOK_WIDTH + hd * HEAD_DIM:TOK_WIDTH + (hd + 1) * HEAD_DIM] = o.astype(BF16)


def _mix_out_gmlp_kernel(u_ref, v_ref, qm_ref, g_v_ref, ws_ref, bs_ref, g_mq_ref, kt_ref, mv_ref,
                         w_out_ref, x_ref, o_ref, mixed_ref):
    v = v_ref[...].astype(F32)
    vn = (v * _rms_scale(v) * g_v_ref[...]).astype(BF16)
    for c in range(v.shape[0] // CHUNK):
        rows = slice(c * CHUNK, (c + 1) * CHUNK)
        for g in range(A_GROUPS):
            lanes = slice(g * HEAD_DIM, (g + 1) * HEAD_DIM)
            s = _dot(ws_ref[g], vn[rows, lanes]) + bs_ref[g]
            mixed_ref[rows, lanes] = (u_ref[rows, lanes].astype(F32) * s).astype(BF16)
    _memory_attention_into(mixed_ref, qm_ref, g_mq_ref, kt_ref, mv_ref)
    o_ref[...] = x_ref[...] + _dot(mixed_ref[...], w_out_ref[...])


def _mix_out_attn_kernel(t_ref, qm_ref, g_mq_ref, kt_ref, mv_ref, w_out_ref, x_ref, o_ref, mixed_ref):
    mixed_ref[:, :TOK_WIDTH] = t_ref[...]
    _memory_attention_into(mixed_ref, qm_ref, g_mq_ref, kt_ref, mv_ref)
    o_ref[...] = x_ref[...] + _dot(mixed_ref[...], w_out_ref[...])


def _mix_out(x2d, z, mem_kt, mem_v, g_mq, w_out, seq, *, gmlp=None, tok=None):
    rows = x2d.shape[0]
    bm = MIX_ROWS
    seq_tiles = seq // bm
    qm_col = (z.shape[1] - MEM_WIDTH) // MEM_WIDTH
    shared_specs = [
        pl.BlockSpec((bm, MEM_WIDTH), lambda i: (i, qm_col)),
    ]
    tail_specs = [
        pl.BlockSpec((1, HEAD_DIM), lambda i: (0, 0)),
        pl.BlockSpec((None, MEM_WIDTH, N_MEM), lambda i: (i // seq_tiles, 0, 0)),
        pl.BlockSpec((None, N_MEM, MEM_WIDTH), lambda i: (i // seq_tiles, 0, 0)),
        pl.BlockSpec((D_MODEL, D_MODEL), lambda i: (0, 0)),
        pl.BlockSpec((bm, D_MODEL), lambda i: (i, 0)),
    ]
    tail_args = [g_mq.reshape(1, HEAD_DIM), mem_kt, mem_v, w_out, x2d]
    if gmlp is not None:
        g_v, w_s, b_s = gmlp
        body = _mix_out_gmlp_kernel
        in_specs = [
            pl.BlockSpec((bm, TOK_WIDTH), lambda i: (i, 0)),
            pl.BlockSpec((bm, TOK_WIDTH), lambda i: (i, 1)),
        ] + shared_specs + [
            pl.BlockSpec((1, TOK_WIDTH), lambda i: (0, 0)),
            pl.BlockSpec((A_GROUPS, CHUNK, CHUNK), lambda i: (0, 0, 0)),
            pl.BlockSpec((A_GROUPS, CHUNK, 1), lambda i: (0, 0, 0)),
        ] + tail_specs
        args = [z, z, z, g_v.reshape(1, TOK_WIDTH), w_s, b_s.reshape(A_GROUPS, CHUNK, 1)] + tail_args
        name = "mix_out_gmlp"
    else:
        body = _mix_out_attn_kernel
        in_specs = [pl.BlockSpec((bm, TOK_WIDTH), lambda i: (i, 0))] + shared_specs + tail_specs
        args = [tok, z] + tail_args
        name = "mix_out_attn"
    return pl.pallas_call(
        body,
        grid=(rows // bm,),
        in_specs=in_specs,
        out_specs=pl.BlockSpec((bm, D_MODEL), lambda i: (i, 0)),
        out_shape=jax.ShapeDtypeStruct((rows, D_MODEL), F32),
        scratch_shapes=[pltpu.VMEM((bm, D_MODEL), BF16)],
        compiler_params=_params("parallel"),
        name=name,
    )(*args)


def _ffn_kernel(x_ref, g_ref, wg_ref, wu_ref, wd_ref, o_ref, h_ref):
    j = pl.program_id(1)

    @pl.when(j == 0)
    def _():
        _norm_to_scratch(x_ref, g_ref, h_ref)
        o_ref[...] = x_ref[...]

    h = h_ref[...]
    gate = _dot(h, wg_ref[...])
    up = _dot(h, wu_ref[...])
    act = (jax.nn.silu(gate) * up).astype(BF16)
    o_ref[...] += _dot(act, wd_ref[...])


def _ffn(x2d, g, w_gate_up, w_down):
    rows = x2d.shape[0]
    d_ff = w_down.shape[0]
    bm, bf = FFN_ROWS, FFN_COLS
    ff_tiles = d_ff // bf
    return pl.pallas_call(
        _ffn_kernel,
        grid=(rows // bm, ff_tiles),
        in_specs=[
            pl.BlockSpec((bm, D_MODEL), lambda i, j: (i, 0)),
            pl.BlockSpec((1, D_MODEL), lambda i, j: (0, 0)),
            pl.BlockSpec((D_MODEL, bf), lambda i, j: (0, j)),
            pl.BlockSpec((D_MODEL, bf), lambda i, j: (0, ff_tiles + j)),
            pl.BlockSpec((bf, D_MODEL), lambda i, j: (j, 0)),
        ],
        out_specs=pl.BlockSpec((bm, D_MODEL), lambda i, j: (i, 0)),
        out_shape=jax.ShapeDtypeStruct((rows, D_MODEL), F32),
        scratch_shapes=[pltpu.VMEM((bm, D_MODEL), BF16)],
        compiler_params=_params("parallel", "arbitrary"),
        name="ffn",
    )(x2d, g.reshape(1, D_MODEL), w_gate_up, w_gate_up, w_down)


def _rope_tables(seq):
    n_rows = seq // GRID_W
    rows = jnp.broadcast_to(jnp.arange(n_rows)[:, None], (n_rows, GRID_W)).reshape(seq)
    cols = jnp.broadcast_to(jnp.arange(GRID_W)[None, :], (n_rows, GRID_W)).reshape(seq)
    freqs = ROPE_THETA ** (-jnp.arange(ROPE_PAIRS, dtype=F32) / ROPE_PAIRS)
    ang_r = rows.astype(F32)[:, None] * freqs
    ang_c = cols.astype(F32)[:, None] * freqs
    ang = jnp.concatenate([ang_r, ang_r, ang_c, ang_c], axis=-1)
    cos, sin = jnp.cos(ang), jnp.sin(ang)
    low_half = (jnp.arange(HEAD_DIM) % (2 * ROPE_PAIRS)) < ROPE_PAIRS
    sin_lo = jnp.where(low_half, -sin, 0.0)
    sin_hi = jnp.where(low_half, 0.0, sin)
    return cos, sin_lo, sin_hi


def kernel(x, mem, g_mix, g_ffn, w_in_a, g_v_a, w_spatial, b_spatial, w_in_b, g_q_b, g_k_b, g_mem, w_mem_kv,
           g_mq, g_mk, w_out, w_gate_up, w_down):
    batch, seq, _ = x.shape
    rows = batch * seq
    assert seq % IN_PROJ_ROWS == 0 and seq % MIX_ROWS == 0 and seq % ATTN_Q_ROWS == 0 and rows % FFN_ROWS == 0
    assert MIX_ROWS % CHUNK == 0 and w_down.shape[1] % FFN_COLS == 0

    cos, sin_lo, sin_hi = _rope_tables(seq)
    mem_kt, mem_v = _mem_kv(mem, g_mem, w_mem_kv.astype(BF16), g_mk)
    xs = x.reshape(rows, D_MODEL)
    for l in range(DEPTH):
        idx = l // 2
        if l % 2 == 0:
            z = _in_proj(xs, g_mix[l], w_in_a[idx].astype(BF16))
            xs = _mix_out(xs, z, mem_kt[l], mem_v[l], g_mq[l], w_out[l].astype(BF16), seq,
                          gmlp=(g_v_a[idx], w_spatial[idx].astype(BF16), b_spatial[idx]))
        else:
            z = _in_proj(xs, g_mix[l], w_in_b[idx].astype(BF16),
                         attn=(g_q_b[idx], g_k_b[idx], cos, sin_lo, sin_hi))
            tok = _attention(z, batch, seq)
            xs = _mix_out(xs, z, mem_kt[l], mem_v[l], g_mq[l], w_out[l].astype(BF16), seq, tok=tok)
        xs = _ffn(xs, g_ffn[l], w_gate_up[l].astype(BF16), w_down[l].astype(BF16))
    return xs.reshape(batch, seq, D_MODEL)
```

```python
import functools

import jax
import jax.numpy as jnp
from jax import lax
from jax.experimental import pallas as pl
from jax.experimental.pallas import tpu as pltpu

D_MODEL = 2048
DEPTH = 4
N_MEM = 256
GRID_W = 64
HEAD_DIM = 128
MEM_HEADS = 4
MEM_WIDTH = MEM_HEADS * HEAD_DIM
TOK_WIDTH = D_MODEL - MEM_WIDTH
CHUNK = 128
A_GROUPS = TOK_WIDTH // HEAD_DIM
Q_HEADS = TOK_WIDTH // HEAD_DIM
KV_HEADS = 4
Q_PER_KV = Q_HEADS // KV_HEADS
KV_WIDTH = KV_HEADS * HEAD_DIM
ROPE_THETA = 10000.0
ROPE_PAIRS = HEAD_DIM // 4
EPS = 1e-6
SCORE_SCALE = HEAD_DIM ** -0.5

VMEM_LIMIT_BYTES = 63 * 1024 * 1024
MXU_COLS = 256
LOG2_E = 1.4426950408889634

IN_PROJ_ROWS = 1024
IN_PROJ_COLS_GMLP = 512
IN_PROJ_COLS_ATTN = 1024
MIX_ROWS = 512
ATTN_Q_ROWS = 256
FFN_ROWS = 1024
FFN_COLS = 512

BF16 = jnp.bfloat16
F32 = jnp.float32


def _rms_scale(x):
    return lax.rsqrt(jnp.mean(x * x, axis=-1, keepdims=True) + EPS)


def _gelu_exact(x):
    return 0.5 * x * (1.0 + lax.erf(x * (2.0 ** -0.5)))


def _dot(a, b):
    return jnp.dot(a, b, preferred_element_type=F32)


def _params(*semantics):
    return pltpu.CompilerParams(dimension_semantics=semantics, vmem_limit_bytes=VMEM_LIMIT_BYTES)


def _mem_kv_kernel(mem_ref, g_mem_ref, w_ref, g_mk_ref, kt_ref, v_ref):
    m = mem_ref[...]
    h = (m * _rms_scale(m) * g_mem_ref[...]).astype(BF16)
    kv = _dot(h, w_ref[...])
    for hd in range(MEM_HEADS):
        k = kv[:, hd * HEAD_DIM:(hd + 1) * HEAD_DIM]
        k = k * _rms_scale(k) * g_mk_ref[...]
        kt_ref[hd * HEAD_DIM:(hd + 1) * HEAD_DIM, :] = k.T.astype(BF16)
    v_ref[...] = kv[:, MEM_WIDTH:].astype(BF16)


def _mem_kv(mem, g_mem, w_mem_kv, g_mk):
    batch = mem.shape[0]
    return pl.pallas_call(
        _mem_kv_kernel,
        grid=(DEPTH, batch),
        in_specs=[
            pl.BlockSpec((None, N_MEM, D_MODEL), lambda l, b: (b, 0, 0)),
            pl.BlockSpec((None, 1, D_MODEL), lambda l, b: (l, 0, 0)),
            pl.BlockSpec((None, D_MODEL, 2 * MEM_WIDTH), lambda l, b: (l, 0, 0)),
            pl.BlockSpec((None, 1, HEAD_DIM), lambda l, b: (l, 0, 0)),
        ],
        out_specs=[
            pl.BlockSpec((None, None, MEM_WIDTH, N_MEM), lambda l, b: (l, b, 0, 0)),
            pl.BlockSpec((None, None, N_MEM, MEM_WIDTH), lambda l, b: (l, b, 0, 0)),
        ],
        out_shape=[
            jax.ShapeDtypeStruct((DEPTH, batch, MEM_WIDTH, N_MEM), BF16),
            jax.ShapeDtypeStruct((DEPTH, batch, N_MEM, MEM_WIDTH), BF16),
        ],
        compiler_params=_params("arbitrary", "arbitrary"),
        name="mem_kv",
    )(mem, g_mem.reshape(DEPTH, 1, D_MODEL), w_mem_kv, g_mk.reshape(DEPTH, 1, HEAD_DIM))


def _norm_to_scratch(x_ref, g_ref, h_ref):
    x = x_ref[...]
    h_ref[...] = (x * _rms_scale(x) * g_ref[...]).astype(BF16)


def _column_chunks(width):
    return [slice(c, c + MXU_COLS) for c in range(0, width, MXU_COLS)]


def _in_proj_gmlp_kernel(x_ref, g_ref, w_ref, z_ref, h_ref, *, gelu_tiles):
    j = pl.program_id(1)

    @pl.when(j == 0)
    def _():
        _norm_to_scratch(x_ref, g_ref, h_ref)

    @pl.when(j < gelu_tiles)
    def _():
        h = h_ref[...]
        for cols in _column_chunks(z_ref.shape[1]):
            z_ref[:, cols] = _gelu_exact(_dot(h, w_ref[:, cols])).astype(BF16)

    @pl.when(j >= gelu_tiles)
    def _():
        z_ref[...] = _dot(h_ref[...], w_ref[...]).astype(BF16)


def _in_proj_attn_kernel(x_ref, g_ref, w_ref, gain_ref, cos_ref, sin_ref, seg_ref, z_ref, h_ref, *, rope_tiles):
    j = pl.program_id(1)

    @pl.when(j == 0)
    def _():
        _norm_to_scratch(x_ref, g_ref, h_ref)

    @pl.when(j < rope_tiles)
    def _():
        h = h_ref[...]
        cos, sin = cos_ref[...], sin_ref[...]
        dot_cols = 2 * MXU_COLS
        for c0 in range(0, z_ref.shape[1], dot_cols):
            acc = _dot(h, w_ref[:, c0:c0 + dot_cols])
            for p0 in range(0, dot_cols, 2 * HEAD_DIM):
                t0, t1 = acc[:, p0:p0 + HEAD_DIM], acc[:, p0 + HEAD_DIM:p0 + 2 * HEAD_DIM]
                lanes0 = slice(c0 + p0, c0 + p0 + HEAD_DIM)
                lanes1 = slice(c0 + p0 + HEAD_DIM, c0 + p0 + 2 * HEAD_DIM)
                ss = t0 * t0 + t1 * t1
                hi = ss.astype(BF16)
                lo = (ss - hi.astype(F32)).astype(BF16)
                tot = _dot(jnp.concatenate([hi, lo], axis=1), seg_ref[...])
                r = lax.rsqrt(tot * (1.0 / HEAD_DIM) + EPS)
                n0 = t0 * r * gain_ref[:, lanes0]
                n1 = t1 * r * gain_ref[:, lanes1]
                z_ref[:, lanes0] = (n0 * cos - n1 * sin).astype(BF16)
                z_ref[:, lanes1] = (n1 * cos + n0 * sin).astype(BF16)

    @pl.when(j >= rope_tiles)
    def _():
        z_ref[...] = _dot(h_ref[...], w_ref[...]).astype(BF16)


def _in_proj(x2d, g, w, *, attn=None):
    rows, _ = x2d.shape
    cols = w.shape[1]
    bm = IN_PROJ_ROWS
    bn = IN_PROJ_COLS_GMLP if attn is None else IN_PROJ_COLS_ATTN
    assert cols % bn == 0 and bn % MXU_COLS == 0
    grid = (rows // bm, cols // bn)
    in_specs = [
        pl.BlockSpec((bm, D_MODEL), lambda i, j: (i, 0)),
        pl.BlockSpec((1, D_MODEL), lambda i, j: (0, 0)),
        pl.BlockSpec((D_MODEL, bn), lambda i, j: (0, j)),
    ]
    args = [x2d, g.reshape(1, D_MODEL), w]
    if attn is None:
        assert (2 * TOK_WIDTH) % bn == 0
        body = functools.partial(_in_proj_gmlp_kernel, gelu_tiles=2 * TOK_WIDTH // bn)
    else:
        gain, cos, sin, seg = attn
        seq_tiles = cos.shape[0] // bm
        rope_width = TOK_WIDTH + KV_WIDTH
        assert rope_width % bn == 0
        body = functools.partial(_in_proj_attn_kernel, rope_tiles=rope_width // bn)
        in_specs += [
            pl.BlockSpec((1, bn), lambda i, j: (0, j)),
            pl.BlockSpec((bm, HEAD_DIM), lambda i, j: (i % seq_tiles, 0)),
            pl.BlockSpec((bm, HEAD_DIM), lambda i, j: (i % seq_tiles, 0)),
            pl.BlockSpec((2 * HEAD_DIM, HEAD_DIM), lambda i, j: (0, 0)),
        ]
        args += [gain, cos, sin, seg]
    return pl.pallas_call(
        body,
        grid=grid,
        in_specs=in_specs,
        out_specs=pl.BlockSpec((bm, bn), lambda i, j: (i, j)),
        out_shape=jax.ShapeDtypeStruct((rows, cols), BF16),
        scratch_shapes=[pltpu.VMEM((bm, D_MODEL), BF16)],
        compiler_params=_params("parallel", "arbitrary"),
        name="in_proj_gmlp" if attn is None else "in_proj_attn",
    )(*args)


_NT_DIMS = (((1,), (1,)), ((), ()))


def _attention_kernel(q_ref, k_ref, v_ref, o_ref):
    k = k_ref[...]
    v = v_ref[...]
    half = HEAD_DIM // 2
    lane = lax.broadcasted_iota(jnp.int32, (q_ref.shape[0], 2 * HEAD_DIM), 1) % HEAD_DIM
    first_lane = (pl.program_id(1) % 2) * half
    keep = jnp.logical_and(lane >= first_lane, lane < first_lane + half)
    for g in range(Q_PER_KV):
        q = q_ref[:, g * 2 * HEAD_DIM:(g + 1) * 2 * HEAD_DIM]
        q = jnp.where(keep, q, jnp.zeros_like(q))
        s = lax.dot_general(q, k, _NT_DIMS, preferred_element_type=F32)
        p = jnp.exp2(s - jnp.max(s, axis=-1, keepdims=True))
        denom = jnp.sum(p, axis=-1, keepdims=True)
        o = _dot(p.astype(BF16), v) / denom
        o_ref[:, g * HEAD_DIM:(g + 1) * HEAD_DIM] = o.astype(BF16)


def _attention(z, batch, seq):
    bq = ATTN_Q_ROWS
    q_tiles = seq // bq
    pair_width = 2 * HEAD_DIM
    k_block0 = TOK_WIDTH // pair_width
    v_block0 = (TOK_WIDTH + KV_WIDTH) // HEAD_DIM
    return pl.pallas_call(
        _attention_kernel,
        grid=(batch, KV_HEADS, q_tiles),
        in_specs=[
            pl.BlockSpec((bq, Q_PER_KV * pair_width), lambda b, h, i: (b * q_tiles + i, h // 2)),
            pl.BlockSpec((seq, pair_width), lambda b, h, i: (b, k_block0 + h // 2)),
            pl.BlockSpec((seq, HEAD_DIM), lambda b, h, i: (b, v_block0 + h)),
        ],
        out_specs=pl.BlockSpec((bq, Q_PER_KV * HEAD_DIM), lambda b, h, i: (b * q_tiles + i, h)),
        out_shape=jax.ShapeDtypeStruct((batch * seq, TOK_WIDTH), BF16),
        compiler_params=_params("parallel", "parallel", "arbitrary"),
        name="attention",
    )(z, z, z)


def _memory_attention_into(mixed_ref, qm_ref, g_mq_ref, kt_ref, v_ref):
    for hd in range(MEM_HEADS):
        lanes = slice(hd * HEAD_DIM, (hd + 1) * HEAD_DIM)
        q = qm_ref[:, lanes].astype(F32)
        q = (q * _rms_scale(q) * g_mq_ref[...] * SCORE_SCALE).astype(BF16)
        s = _dot(q, kt_ref[lanes, :])
        p = jnp.exp(s - jnp.max(s, axis=-1, keepdims=True))
        denom = jnp.sum(p, axis=-1, keepdims=True)
        o = _dot(p.astype(BF16), v_ref[:, lanes]) / denom
        mixed_ref[:, TOK_WIDTH + hd * HEAD_DIM:TOK_WIDTH + (hd + 1) * HEAD_DIM] = o.astype(BF16)


def _mix_out_gmlp_kernel(u_ref, v_ref, qm_ref, g_v_ref, ws_ref, bs_ref, g_mq_ref, kt_ref, mv_ref,
                         w_out_ref, x_ref, o_ref, mixed_ref):
    v = v_ref[...].astype(F32)
    vn = (v * _rms_scale(v) * g_v_ref[...]).astype(BF16)
    for c in range(v.shape[0] // CHUNK):
        rows = slice(c * CHUNK, (c + 1) * CHUNK)
        for g in range(A_GROUPS):
            lanes = slice(g * HEAD_DIM, (g + 1) * HEAD_DIM)
            s = _dot(ws_ref[g], vn[rows, lanes]) + bs_ref[g]
            mixed_ref[rows, lanes] = (u_ref[rows, lanes].astype(F32) * s).astype(BF16)
    _memory_attention_into(mixed_ref, qm_ref, g_mq_ref, kt_ref, mv_ref)
    o_ref[...] = x_ref[...] + _dot(mixed_ref[...], w_out_ref[...])


def _mix_out_attn_kernel(t_ref, qm_ref, g_mq_ref, kt_ref, mv_ref, w_out_ref, x_ref, o_ref, mixed_ref):
    mixed_ref[:, :TOK_WIDTH] = t_ref[...]
    _memory_attention_into(mixed_ref, qm_ref, g_mq_ref, kt_ref, mv_ref)
    o_ref[...] = x_ref[...] + _dot(mixed_ref[...], w_out_ref[...])


def _mix_out(x2d, z, mem_kt, mem_v, g_mq, w_out, seq, *, gmlp=None, tok=None):
    rows = x2d.shape[0]
    bm = MIX_ROWS
    seq_tiles = seq // bm
    qm_col = (z.shape[1] - MEM_WIDTH) // MEM_WIDTH
    shared_specs = [
        pl.BlockSpec((bm, MEM_WIDTH), lambda i: (i, qm_col)),
    ]
    tail_specs = [
        pl.BlockSpec((1, HEAD_DIM), lambda i: (0, 0)),
        pl.BlockSpec((None, MEM_WIDTH, N_MEM), lambda i: (i // seq_tiles, 0, 0)),
        pl.BlockSpec((None, N_MEM, MEM_WIDTH), lambda i: (i // seq_tiles, 0, 0)),
        pl.BlockSpec((D_MODEL, D_MODEL), lambda i: (0, 0), pipeline_mode=pl.Buffered(1)),
        pl.BlockSpec((bm, D_MODEL), lambda i: (i, 0)),
    ]
    tail_args = [g_mq.reshape(1, HEAD_DIM), mem_kt, mem_v, w_out, x2d]
    if gmlp is not None:
        g_v, w_s, b_s = gmlp
        body = _mix_out_gmlp_kernel
        in_specs = [
            pl.BlockSpec((bm, TOK_WIDTH), lambda i: (i, 0)),
            pl.BlockSpec((bm, TOK_WIDTH), lambda i: (i, 1)),
        ] + shared_specs + [
            pl.BlockSpec((1, TOK_WIDTH), lambda i: (0, 0)),
            pl.BlockSpec((A_GROUPS, CHUNK, CHUNK), lambda i: (0, 0, 0)),
            pl.BlockSpec((A_GROUPS, CHUNK, 1), lambda i: (0, 0, 0)),
        ] + tail_specs
        args = [z, z, z, g_v.reshape(1, TOK_WIDTH), w_s, b_s.reshape(A_GROUPS, CHUNK, 1)] + tail_args
        name = "mix_out_gmlp"
    else:
        body = _mix_out_attn_kernel
        in_specs = [pl.BlockSpec((bm, TOK_WIDTH), lambda i: (i, 0))] + shared_specs + tail_specs
        args = [tok, z] + tail_args
        name = "mix_out_attn"
    return pl.pallas_call(
        body,
        grid=(rows // bm,),
        in_specs=in_specs,
        out_specs=pl.BlockSpec((bm, D_MODEL), lambda i: (i, 0)),
        out_shape=jax.ShapeDtypeStruct((rows, D_MODEL), F32),
        scratch_shapes=[pltpu.VMEM((bm, D_MODEL), BF16)],
        compiler_params=_params("parallel"),
        name=name,
    )(*args)


def _ffn_kernel(x_ref, g_ref, wg_ref, wu_ref, wd_ref, o_ref, h_ref):
    j = pl.program_id(1)

    @pl.when(j == 0)
    def _():
        _norm_to_scratch(x_ref, g_ref, h_ref)
        o_ref[...] = x_ref[...]

    h = h_ref[...]
    gate = _dot(h, wg_ref[...])
    up = _dot(h, wu_ref[...])
    act = (jax.nn.silu(gate) * up).astype(BF16)
    o_ref[...] += _dot(act, wd_ref[...])


def _ffn(x2d, g, w_gate_up, w_down):
    rows = x2d.shape[0]
    d_ff = w_down.shape[0]
    bm, bf = FFN_ROWS, FFN_COLS
    ff_tiles = d_ff // bf
    return pl.pallas_call(
        _ffn_kernel,
        grid=(rows // bm, ff_tiles),
        in_specs=[
            pl.BlockSpec((bm, D_MODEL), lambda i, j: (i, 0)),
            pl.BlockSpec((1, D_MODEL), lambda i, j: (0, 0)),
            pl.BlockSpec((D_MODEL, bf), lambda i, j: (0, j)),
            pl.BlockSpec((D_MODEL, bf), lambda i, j: (0, ff_tiles + j)),
            pl.BlockSpec((bf, D_MODEL), lambda i, j: (j, 0)),
        ],
        out_specs=pl.BlockSpec((bm, D_MODEL), lambda i, j: (i, 0)),
        out_shape=jax.ShapeDtypeStruct((rows, D_MODEL), F32),
        scratch_shapes=[pltpu.VMEM((bm, D_MODEL), BF16)],
        compiler_params=_params("parallel", "arbitrary"),
        name="ffn",
    )(x2d, g.reshape(1, D_MODEL), w_gate_up, w_gate_up, w_down)


def _pair_layout_columns(a, kv_groups):
    lead = a.shape[:-1]
    n = len(lead)
    a = a.reshape(lead + (KV_HEADS // 2, 2, kv_groups, 2, 2, ROPE_PAIRS))
    a = a.transpose(tuple(range(n)) + (n, n + 2, n + 4, n + 1, n + 3, n + 5))
    return a.reshape(lead + (KV_HEADS * kv_groups * HEAD_DIM,))


def _pair_layout_weights(w):
    q, k, rest = w[:, :TOK_WIDTH], w[:, TOK_WIDTH:TOK_WIDTH + KV_WIDTH], w[:, TOK_WIDTH + KV_WIDTH:]
    return jnp.concatenate([_pair_layout_columns(q, Q_PER_KV), _pair_layout_columns(k, 1), rest], axis=1)


def _attn_tables(seq):
    n_rows = seq // GRID_W
    rows = jnp.broadcast_to(jnp.arange(n_rows)[:, None], (n_rows, GRID_W)).reshape(seq)
    cols = jnp.broadcast_to(jnp.arange(GRID_W)[None, :], (n_rows, GRID_W)).reshape(seq)
    freqs = ROPE_THETA ** (-jnp.arange(ROPE_PAIRS, dtype=F32) / ROPE_PAIRS)
    ang_r = rows.astype(F32)[:, None] * freqs
    ang_c = cols.astype(F32)[:, None] * freqs
    ang = jnp.concatenate([ang_r, ang_c, ang_r, ang_c], axis=-1)
    k_seg = (jnp.arange(2 * HEAD_DIM) % HEAD_DIM) // (HEAD_DIM // 2)
    n_seg = jnp.arange(HEAD_DIM) // (HEAD_DIM // 2)
    seg = (k_seg[:, None] == n_seg[None, :]).astype(BF16)
    return jnp.cos(ang), jnp.sin(ang), seg


def _attn_gains(g_q, g_k, cols):
    gq = _pair_layout_columns(jnp.tile(g_q * (SCORE_SCALE * LOG2_E), Q_HEADS), Q_PER_KV)
    gk = _pair_layout_columns(jnp.tile(g_k, KV_HEADS), 1)
    return jnp.concatenate([gq, gk, jnp.ones((cols - TOK_WIDTH - KV_WIDTH,), F32)]).reshape(1, cols)


def kernel(x, mem, g_mix, g_ffn, w_in_a, g_v_a, w_spatial, b_spatial, w_in_b, g_q_b, g_k_b, g_mem, w_mem_kv,
           g_mq, g_mk, w_out, w_gate_up, w_down):
    batch, seq, _ = x.shape
    rows = batch * seq
    assert seq % IN_PROJ_ROWS == 0 and seq % MIX_ROWS == 0 and seq % ATTN_Q_ROWS == 0 and rows % FFN_ROWS == 0
    assert MIX_ROWS % CHUNK == 0 and w_down.shape[1] % FFN_COLS == 0

    cos, sin, seg = _attn_tables(seq)
    mem_kt, mem_v = _mem_kv(mem, g_mem, w_mem_kv.astype(BF16), g_mk)
    xs = x.reshape(rows, D_MODEL)
    for l in range(DEPTH):
        idx = l // 2
        if l % 2 == 0:
            z = _in_proj(xs, g_mix[l], w_in_a[idx].astype(BF16))
            xs = _mix_out(xs, z, mem_kt[l], mem_v[l], g_mq[l], w_out[l].astype(BF16), seq,
                          gmlp=(g_v_a[idx], w_spatial[idx].astype(BF16), b_spatial[idx]))
        else:
            w_in = _pair_layout_weights(w_in_b[idx].astype(BF16))
            gain = _attn_gains(g_q_b[idx], g_k_b[idx], w_in.shape[1])
            z = _in_proj(xs, g_mix[l], w_in, attn=(gain, cos, sin, seg))
            tok = _attention(z, batch, seq)
            xs = _mix_out(xs, z, mem_kt[l], mem_v[l], g_mq[l], w_out[l].astype(BF16), seq, tok=tok)
        xs = _ffn(xs, g_ffn[l], w_gate_up[l].astype(BF16), w_down[l].astype(BF16))
    return xs.reshape(batch, seq, D_MODEL)
```

```python
import functools

import jax
import jax.numpy as jnp
from jax import lax
from jax.experimental import pallas as pl
from jax.experimental.pallas import tpu as pltpu

D_MODEL = 2048
DEPTH = 4
N_MEM = 256
GRID_W = 64
HEAD_DIM = 128
MEM_HEADS = 4
MEM_WIDTH = MEM_HEADS * HEAD_DIM
TOK_WIDTH = D_MODEL - MEM_WIDTH
CHUNK = 128
A_GROUPS = TOK_WIDTH // HEAD_DIM
Q_HEADS = TOK_WIDTH // HEAD_DIM
KV_HEADS = 4
Q_PER_KV = Q_HEADS // KV_HEADS
KV_WIDTH = KV_HEADS * HEAD_DIM
ROPE_THETA = 10000.0
ROPE_PAIRS = HEAD_DIM // 4
EPS = 1e-6
SCORE_SCALE = HEAD_DIM ** -0.5

VMEM_LIMIT_BYTES = 63 * 1024 * 1024
MXU_COLS = 256
LOG2_E = 1.4426950408889634

IN_PROJ_ROWS = 1024
IN_PROJ_COLS_GMLP = 512
IN_PROJ_COLS_ATTN = 1024
MIX_ROWS = 512
ATTN_Q_ROWS = 256
FFN_ROWS = 1024
FFN_COLS = 512

BF16 = jnp.bfloat16
F32 = jnp.float32


def _rms_scale(x):
    return lax.rsqrt(jnp.mean(x * x, axis=-1, keepdims=True) + EPS)


def _gelu_exact(x):
    return 0.5 * x * (1.0 + lax.erf(x * (2.0 ** -0.5)))


def _dot(a, b):
    return jnp.dot(a, b, preferred_element_type=F32)


def _params(*semantics):
    return pltpu.CompilerParams(dimension_semantics=semantics, vmem_limit_bytes=VMEM_LIMIT_BYTES)


def _mem_kv_kernel(mem_ref, g_mem_ref, w_ref, g_mk_ref, kt_ref, v_ref):
    m = mem_ref[...]
    h = (m * _rms_scale(m) * g_mem_ref[...]).astype(BF16)
    kv = _dot(h, w_ref[...])
    for hd in range(MEM_HEADS):
        k = kv[:, hd * HEAD_DIM:(hd + 1) * HEAD_DIM]
        k = k * _rms_scale(k) * g_mk_ref[...]
        kt_ref[hd * HEAD_DIM:(hd + 1) * HEAD_DIM, :] = k.T.astype(BF16)
    v_ref[...] = kv[:, MEM_WIDTH:].astype(BF16)


def _mem_kv(mem, g_mem, w_mem_kv, g_mk):
    batch = mem.shape[0]
    return pl.pallas_call(
        _mem_kv_kernel,
        grid=(DEPTH, batch),
        in_specs=[
            pl.BlockSpec((None, N_MEM, D_MODEL), lambda l, b: (b, 0, 0)),
            pl.BlockSpec((None, 1, D_MODEL), lambda l, b: (l, 0, 0)),
            pl.BlockSpec((None, D_MODEL, 2 * MEM_WIDTH), lambda l, b: (l, 0, 0)),
            pl.BlockSpec((None, 1, HEAD_DIM), lambda l, b: (l, 0, 0)),
        ],
        out_specs=[
            pl.BlockSpec((None, None, MEM_WIDTH, N_MEM), lambda l, b: (l, b, 0, 0)),
            pl.BlockSpec((None, None, N_MEM, MEM_WIDTH), lambda l, b: (l, b, 0, 0)),
        ],
        out_shape=[
            jax.ShapeDtypeStruct((DEPTH, batch, MEM_WIDTH, N_MEM), BF16),
            jax.ShapeDtypeStruct((DEPTH, batch, N_MEM, MEM_WIDTH), BF16),
        ],
        compiler_params=_params("arbitrary", "arbitrary"),
        name="mem_kv",
    )(mem, g_mem.reshape(DEPTH, 1, D_MODEL), w_mem_kv, g_mk.reshape(DEPTH, 1, HEAD_DIM))


def _norm_to_scratch(x_ref, g_ref, h_ref):
    x = x_ref[...]
    h_ref[...] = (x * _rms_scale(x) * g_ref[...]).astype(BF16)


def _column_chunks(width):
    return [slice(c, c + MXU_COLS) for c in range(0, width, MXU_COLS)]


def _in_proj_gmlp_kernel(x_ref, g_ref, w_ref, z_ref, h_ref, *, gelu_tiles):
    j = pl.program_id(1)

    @pl.when(j == 0)
    def _():
        _norm_to_scratch(x_ref, g_ref, h_ref)

    @pl.when(j < gelu_tiles)
    def _():
        h = h_ref[...]
        for cols in _column_chunks(z_ref.shape[1]):
            z_ref[:, cols] = _gelu_exact(_dot(h, w_ref[:, cols])).astype(BF16)

    @pl.when(j >= gelu_tiles)
    def _():
        z_ref[...] = _dot(h_ref[...], w_ref[...]).astype(BF16)


def _in_proj_attn_kernel(x_ref, g_ref, w_ref, gain_ref, cos_ref, sin_ref, seg_ref, z_ref, h_ref, *, rope_tiles):
    j = pl.program_id(1)

    @pl.when(j == 0)
    def _():
        _norm_to_scratch(x_ref, g_ref, h_ref)

    @pl.when(j < rope_tiles)
    def _():
        h = h_ref[...]
        cos, sin = cos_ref[...], sin_ref[...]
        dot_cols = 2 * MXU_COLS
        for c0 in range(0, z_ref.shape[1], dot_cols):
            acc = _dot(h, w_ref[:, c0:c0 + dot_cols])
            for p0 in range(0, dot_cols, 2 * HEAD_DIM):
                t0, t1 = acc[:, p0:p0 + HEAD_DIM], acc[:, p0 + HEAD_DIM:p0 + 2 * HEAD_DIM]
                lanes0 = slice(c0 + p0, c0 + p0 + HEAD_DIM)
                lanes1 = slice(c0 + p0 + HEAD_DIM, c0 + p0 + 2 * HEAD_DIM)
                ss = t0 * t0 + t1 * t1
                hi = ss.astype(BF16)
                lo = (ss - hi.astype(F32)).astype(BF16)
                tot = _dot(jnp.concatenate([hi, lo], axis=1), seg_ref[...])
                r = lax.rsqrt(tot * (1.0 / HEAD_DIM) + EPS)
                n0 = t0 * r * gain_ref[:, lanes0]
                n1 = t1 * r * gain_ref[:, lanes1]
                z_ref[:, lanes0] = (n0 * cos - n1 * sin).astype(BF16)
                z_ref[:, lanes1] = (n1 * cos + n0 * sin).astype(BF16)

    @pl.when(j >= rope_tiles)
    def _():
        z_ref[...] = _dot(h_ref[...], w_ref[...]).astype(BF16)


def _in_proj(x2d, g, w, layer, *, attn=None):
    rows, _ = x2d.shape
    cols = w.shape[2]
    bm = IN_PROJ_ROWS
    bn = IN_PROJ_COLS_GMLP if attn is None else IN_PROJ_COLS_ATTN
    assert cols % bn == 0 and bn % MXU_COLS == 0
    grid = (rows // bm, cols // bn)
    in_specs = [
        pl.BlockSpec((bm, D_MODEL), lambda i, j: (i, 0)),
        pl.BlockSpec((1, D_MODEL), lambda i, j: (0, 0)),
        pl.BlockSpec((None, D_MODEL, bn), lambda i, j: (layer, 0, j)),
    ]
    args = [x2d, g.reshape(1, D_MODEL), w]
    if attn is None:
        assert (2 * TOK_WIDTH) % bn == 0
        body = functools.partial(_in_proj_gmlp_kernel, gelu_tiles=2 * TOK_WIDTH // bn)
    else:
        gain, cos, sin, seg = attn
        seq_tiles = cos.shape[0] // bm
        rope_width = TOK_WIDTH + KV_WIDTH
        assert rope_width % bn == 0
        body = functools.partial(_in_proj_attn_kernel, rope_tiles=rope_width // bn)
        in_specs += [
            pl.BlockSpec((1, bn), lambda i, j: (0, j)),
            pl.BlockSpec((bm, HEAD_DIM), lambda i, j: (i % seq_tiles, 0)),
            pl.BlockSpec((bm, HEAD_DIM), lambda i, j: (i % seq_tiles, 0)),
            pl.BlockSpec((2 * HEAD_DIM, HEAD_DIM), lambda i, j: (0, 0)),
        ]
        args += [gain, cos, sin, seg]
    return pl.pallas_call(
        body,
        grid=grid,
        in_specs=in_specs,
        out_specs=pl.BlockSpec((bm, bn), lambda i, j: (i, j)),
        out_shape=jax.ShapeDtypeStruct((rows, cols), BF16),
        scratch_shapes=[pltpu.VMEM((bm, D_MODEL), BF16)],
        compiler_params=_params("parallel", "arbitrary"),
        name="in_proj_gmlp" if attn is None else "in_proj_attn",
    )(*args)


_NT_DIMS = (((1,), (1,)), ((), ()))


def _attention_kernel(q_ref, k_ref, v_ref, o_ref):
    k = k_ref[...]
    v = v_ref[...]
    half = HEAD_DIM // 2
    lane = lax.broadcasted_iota(jnp.int32, (q_ref.shape[0], 2 * HEAD_DIM), 1) % HEAD_DIM
    first_lane = (pl.program_id(1) % 2) * half
    keep = jnp.logical_and(lane >= first_lane, lane < first_lane + half)
    for g in range(Q_PER_KV):
        q = q_ref[:, g * 2 * HEAD_DIM:(g + 1) * 2 * HEAD_DIM]
        q = jnp.where(keep, q, jnp.zeros_like(q))
        s = lax.dot_general(q, k, _NT_DIMS, preferred_element_type=F32)
        p = jnp.exp2(s - jnp.max(s, axis=-1, keepdims=True))
        denom = jnp.sum(p, axis=-1, keepdims=True)
        o = _dot(p.astype(BF16), v) / denom
        o_ref[:, g * HEAD_DIM:(g + 1) * HEAD_DIM] = o.astype(BF16)


def _attention(z, batch, seq):
    bq = ATTN_Q_ROWS
    q_tiles = seq // bq
    pair_width = 2 * HEAD_DIM
    k_block0 = TOK_WIDTH // pair_width
    v_block0 = (TOK_WIDTH + KV_WIDTH) // HEAD_DIM
    return pl.pallas_call(
        _attention_kernel,
        grid=(batch, KV_HEADS, q_tiles),
        in_specs=[
            pl.BlockSpec((bq, Q_PER_KV * pair_width), lambda b, h, i: (b * q_tiles + i, h // 2)),
            pl.BlockSpec((seq, pair_width), lambda b, h, i: (b, k_block0 + h // 2)),
            pl.BlockSpec((seq, HEAD_DIM), lambda b, h, i: (b, v_block0 + h)),
        ],
        out_specs=pl.BlockSpec((bq, Q_PER_KV * HEAD_DIM), lambda b, h, i: (b * q_tiles + i, h)),
        out_shape=jax.ShapeDtypeStruct((batch * seq, TOK_WIDTH), BF16),
        compiler_params=_params("parallel", "parallel", "arbitrary"),
        name="attention",
    )(z, z, z)


def _memory_attention_into(mixed_ref, qm_ref, g_mq_ref, kt_ref, v_ref):
    for hd in range(MEM_HEADS):
        lanes = slice(hd * HEAD_DIM, (hd + 1) * HEAD_DIM)
        q = qm_ref[:, lanes].astype(F32)
        q = (q * _rms_scale(q) * g_mq_ref[...] * SCORE_SCALE).astype(BF16)
        s = _dot(q, kt_ref[lanes, :])
        p = jnp.exp(s - jnp.max(s, axis=-1, keepdims=True))
        denom = jnp.sum(p, axis=-1, keepdims=True)
        o = _dot(p.astype(BF16), v_ref[:, lanes]) / denom
        mixed_ref[:, TOK_WIDTH + hd * HEAD_DIM:TOK_WIDTH + (hd + 1) * HEAD_DIM] = o.astype(BF16)


def _mix_out_gmlp_kernel(u_ref, v_ref, qm_ref, g_v_ref, ws_ref, bs_ref, g_mq_ref, kt_ref, mv_ref,
                         w_out_ref, x_ref, o_ref, mixed_ref):
    v = v_ref[...].astype(F32)
    vn = (v * _rms_scale(v) * g_v_ref[...]).astype(BF16)
    for c in range(v.shape[0] // CHUNK):
        rows = slice(c * CHUNK, (c + 1) * CHUNK)
        for g in range(A_GROUPS):
            lanes = slice(g * HEAD_DIM, (g + 1) * HEAD_DIM)
            s = _dot(ws_ref[g], vn[rows, lanes]) + bs_ref[g]
            mixed_ref[rows, lanes] = (u_ref[rows, lanes].astype(F32) * s).astype(BF16)
    _memory_attention_into(mixed_ref, qm_ref, g_mq_ref, kt_ref, mv_ref)
    o_ref[...] = x_ref[...] + _dot(mixed_ref[...], w_out_ref[...])


def _mix_out_attn_kernel(t_ref, qm_ref, g_mq_ref, kt_ref, mv_ref, w_out_ref, x_ref, o_ref, mixed_ref):
    mixed_ref[:, :TOK_WIDTH] = t_ref[...]
    _memory_attention_into(mixed_ref, qm_ref, g_mq_ref, kt_ref, mv_ref)
    o_ref[...] = x_ref[...] + _dot(mixed_ref[...], w_out_ref[...])


def _mix_out(x2d, z, mem_kt, mem_v, g_mq, w_out, layer, seq, *, gmlp=None, tok=None):
    rows = x2d.shape[0]
    bm = MIX_ROWS
    seq_tiles = seq // bm
    qm_col = (z.shape[1] - MEM_WIDTH) // MEM_WIDTH
    shared_specs = [
        pl.BlockSpec((bm, MEM_WIDTH), lambda i: (i, qm_col)),
    ]
    tail_specs = [
        pl.BlockSpec((1, HEAD_DIM), lambda i: (0, 0)),
        pl.BlockSpec((None, None, MEM_WIDTH, N_MEM), lambda i: (layer, i // seq_tiles, 0, 0)),
        pl.BlockSpec((None, None, N_MEM, MEM_WIDTH), lambda i: (layer, i // seq_tiles, 0, 0)),
        pl.BlockSpec((None, D_MODEL, D_MODEL), lambda i: (layer, 0, 0), pipeline_mode=pl.Buffered(1)),
        pl.BlockSpec((bm, D_MODEL), lambda i: (i, 0)),
    ]
    tail_args = [g_mq.reshape(1, HEAD_DIM), mem_kt, mem_v, w_out, x2d]
    if gmlp is not None:
        g_v, w_s, b_s, gmlp_layer = gmlp
        body = _mix_out_gmlp_kernel
        in_specs = [
            pl.BlockSpec((bm, TOK_WIDTH), lambda i: (i, 0)),
            pl.BlockSpec((bm, TOK_WIDTH), lambda i: (i, 1)),
        ] + shared_specs + [
            pl.BlockSpec((1, TOK_WIDTH), lambda i: (0, 0)),
            pl.BlockSpec((None, A_GROUPS, CHUNK, CHUNK), lambda i: (gmlp_layer, 0, 0, 0)),
            pl.BlockSpec((A_GROUPS, CHUNK, 1), lambda i: (0, 0, 0)),
        ] + tail_specs
        args = [z, z, z, g_v.reshape(1, TOK_WIDTH), w_s, b_s.reshape(A_GROUPS, CHUNK, 1)] + tail_args
        name = "mix_out_gmlp"
    else:
        body = _mix_out_attn_kernel
        in_specs = [pl.BlockSpec((bm, TOK_WIDTH), lambda i: (i, 0))] + shared_specs + tail_specs
        args = [tok, z] + tail_args
        name = "mix_out_attn"
    return pl.pallas_call(
        body,
        grid=(rows // bm,),
        in_specs=in_specs,
        out_specs=pl.BlockSpec((bm, D_MODEL), lambda i: (i, 0)),
        out_shape=jax.ShapeDtypeStruct((rows, D_MODEL), F32),
        scratch_shapes=[pltpu.VMEM((bm, D_MODEL), BF16)],
        compiler_params=_params("parallel"),
        name=name,
    )(*args)


def _ffn_kernel(x_ref, g_ref, wg_ref, wu_ref, wd_ref, o_ref, h_ref):
    j = pl.program_id(1)

    @pl.when(j == 0)
    def _():
        _norm_to_scratch(x_ref, g_ref, h_ref)
        o_ref[...] = x_ref[...]

    h = h_ref[...]
    gate = _dot(h, wg_ref[...])
    up = _dot(h, wu_ref[...])
    act = (jax.nn.silu(gate) * up).astype(BF16)
    o_ref[...] += _dot(act, wd_ref[...])


def _ffn(x2d, g, w_gate_up, w_down, layer):
    rows = x2d.shape[0]
    d_ff = w_down.shape[1]
    bm, bf = FFN_ROWS, FFN_COLS
    ff_tiles = d_ff // bf
    return pl.pallas_call(
        _ffn_kernel,
        grid=(rows // bm, ff_tiles),
        in_specs=[
            pl.BlockSpec((bm, D_MODEL), lambda i, j: (i, 0)),
            pl.BlockSpec((1, D_MODEL), lambda i, j: (0, 0)),
            pl.BlockSpec((None, D_MODEL, bf), lambda i, j: (layer, 0, j)),
            pl.BlockSpec((None, D_MODEL, bf), lambda i, j: (layer, 0, ff_tiles + j)),
            pl.BlockSpec((None, bf, D_MODEL), lambda i, j: (layer, j, 0)),
        ],
        out_specs=pl.BlockSpec((bm, D_MODEL), lambda i, j: (i, 0)),
        out_shape=jax.ShapeDtypeStruct((rows, D_MODEL), F32),
        scratch_shapes=[pltpu.VMEM((bm, D_MODEL), BF16)],
        compiler_params=_params("parallel", "arbitrary"),
        name="ffn",
    )(x2d, g.reshape(1, D_MODEL), w_gate_up, w_gate_up, w_down)


def _pair_layout_columns(a, kv_groups):
    lead = a.shape[:-1]
    n = len(lead)
    a = a.reshape(lead + (KV_HEADS // 2, 2, kv_groups, 2, 2, ROPE_PAIRS))
    a = a.transpose(tuple(range(n)) + (n, n + 2, n + 4, n + 1, n + 3, n + 5))
    return a.reshape(lead + (KV_HEADS * kv_groups * HEAD_DIM,))


def _pair_layout_weights(w):
    q, k, rest = w[..., :TOK_WIDTH], w[..., TOK_WIDTH:TOK_WIDTH + KV_WIDTH], w[..., TOK_WIDTH + KV_WIDTH:]
    return jnp.concatenate([_pair_layout_columns(q, Q_PER_KV), _pair_layout_columns(k, 1), rest], axis=-1)


def _attn_tables(seq):
    n_rows = seq // GRID_W
    rows = jnp.broadcast_to(jnp.arange(n_rows)[:, None], (n_rows, GRID_W)).reshape(seq)
    cols = jnp.broadcast_to(jnp.arange(GRID_W)[None, :], (n_rows, GRID_W)).reshape(seq)
    freqs = ROPE_THETA ** (-jnp.arange(ROPE_PAIRS, dtype=F32) / ROPE_PAIRS)
    ang_r = rows.astype(F32)[:, None] * freqs
    ang_c = cols.astype(F32)[:, None] * freqs
    ang = jnp.concatenate([ang_r, ang_c, ang_r, ang_c], axis=-1)
    k_seg = (jnp.arange(2 * HEAD_DIM) % HEAD_DIM) // (HEAD_DIM // 2)
    n_seg = jnp.arange(HEAD_DIM) // (HEAD_DIM // 2)
    seg = (k_seg[:, None] == n_seg[None, :]).astype(BF16)
    return jnp.cos(ang), jnp.sin(ang), seg


def _attn_gains(g_q, g_k, cols):
    gq = _pair_layout_columns(jnp.tile(g_q * (SCORE_SCALE * LOG2_E), Q_HEADS), Q_PER_KV)
    gk = _pair_layout_columns(jnp.tile(g_k, KV_HEADS), 1)
    return jnp.concatenate([gq, gk, jnp.ones((cols - TOK_WIDTH - KV_WIDTH,), F32)]).reshape(1, cols)


def kernel(x, mem, g_mix, g_ffn, w_in_a, g_v_a, w_spatial, b_spatial, w_in_b, g_q_b, g_k_b, g_mem, w_mem_kv,
           g_mq, g_mk, w_out, w_gate_up, w_down):
    batch, seq, _ = x.shape
    rows = batch * seq
    assert seq % IN_PROJ_ROWS == 0 and seq % MIX_ROWS == 0 and seq % ATTN_Q_ROWS == 0 and rows % FFN_ROWS == 0
    assert MIX_ROWS % CHUNK == 0 and w_down.shape[1] % FFN_COLS == 0

    cos, sin, seg = _attn_tables(seq)
    mem_kt, mem_v = _mem_kv(mem, g_mem, w_mem_kv.astype(BF16), g_mk)
    w_in_a, w_spatial, w_out = w_in_a.astype(BF16), w_spatial.astype(BF16), w_out.astype(BF16)
    w_in_b = _pair_layout_weights(w_in_b.astype(BF16))
    w_gate_up, w_down = w_gate_up.astype(BF16), w_down.astype(BF16)
    xs = x.reshape(rows, D_MODEL)
    for l in range(DEPTH):
        idx = l // 2
        if l % 2 == 0:
            z = _in_proj(xs, g_mix[l], w_in_a, idx)
            xs = _mix_out(xs, z, mem_kt, mem_v, g_mq[l], w_out, l, seq,
                          gmlp=(g_v_a[idx], w_spatial, b_spatial[idx], idx))
        else:
            gain = _attn_gains(g_q_b[idx], g_k_b[idx], w_in_b.shape[2])
            z = _in_proj(xs, g_mix[l], w_in_b, idx, attn=(gain, cos, sin, seg))
            tok = _attention(z, batch, seq)
            xs = _mix_out(xs, z, mem_kt, mem_v, g_mq[l], w_out, l, seq, tok=tok)
        xs = _ffn(xs, g_ffn[l], w_gate_up, w_down, l)
    return xs.reshape(batch, seq, D_MODEL)
```

```python
import functools

import jax
import jax.numpy as jnp
from jax import lax
from jax.experimental import pallas as pl
from jax.experimental.pallas import tpu as pltpu

D_MODEL = 2048
DEPTH = 4
N_MEM = 256
GRID_W = 64
HEAD_DIM = 128
MEM_HEADS = 4
MEM_WIDTH = MEM_HEADS * HEAD_DIM
TOK_WIDTH = D_MODEL - MEM_WIDTH
CHUNK = 128
A_GROUPS = TOK_WIDTH // HEAD_DIM
Q_HEADS = TOK_WIDTH // HEAD_DIM
KV_HEADS = 4
Q_PER_KV = Q_HEADS // KV_HEADS
KV_WIDTH = KV_HEADS * HEAD_DIM
ROPE_THETA = 10000.0
ROPE_PAIRS = HEAD_DIM // 4
EPS = 1e-6
SCORE_SCALE = HEAD_DIM ** -0.5

VMEM_LIMIT_BYTES = 63 * 1024 * 1024
MXU_COLS = 256
LOG2_E = 1.4426950408889634

IN_PROJ_ROWS = 1024
IN_PROJ_COLS_GMLP = 512
IN_PROJ_COLS_ATTN = 1024
MIX_ROWS = 512
ATTN_Q_ROWS = 1024
ATTN_CHAIN_ROWS = 256
FFN_ROWS = 1024
FFN_COLS = 512
NORM_CHUNK_ROWS = 256

BF16 = jnp.bfloat16
F32 = jnp.float32


def _rms_scale(x):
    return lax.rsqrt(jnp.mean(x * x, axis=-1, keepdims=True) + EPS)


def _gelu_exact(x):
    return 0.5 * x * (1.0 + lax.erf(x * (2.0 ** -0.5)))


def _dot(a, b):
    return jnp.dot(a, b, preferred_element_type=F32)


def _params(*semantics):
    return pltpu.CompilerParams(dimension_semantics=semantics, vmem_limit_bytes=VMEM_LIMIT_BYTES)


def _mem_kv_kernel(mem_ref, g_mem_ref, w_ref, g_mk_ref, kt_ref, v_ref):
    m = mem_ref[...]
    h = (m * _rms_scale(m) * g_mem_ref[...]).astype(BF16)
    kv = _dot(h, w_ref[...])
    for hd in range(MEM_HEADS):
        k = kv[:, hd * HEAD_DIM:(hd + 1) * HEAD_DIM]
        k = k * _rms_scale(k) * g_mk_ref[...]
        kt_ref[hd * HEAD_DIM:(hd + 1) * HEAD_DIM, :] = k.T.astype(BF16)
    v_ref[...] = kv[:, MEM_WIDTH:].astype(BF16)


def _mem_kv(mem, g_mem, w_mem_kv, g_mk):
    batch = mem.shape[0]
    return pl.pallas_call(
        _mem_kv_kernel,
        grid=(DEPTH, batch),
        in_specs=[
            pl.BlockSpec((None, N_MEM, D_MODEL), lambda l, b: (b, 0, 0)),
            pl.BlockSpec((None, 1, D_MODEL), lambda l, b: (l, 0, 0)),
            pl.BlockSpec((None, D_MODEL, 2 * MEM_WIDTH), lambda l, b: (l, 0, 0)),
            pl.BlockSpec((None, 1, HEAD_DIM), lambda l, b: (l, 0, 0)),
        ],
        out_specs=[
            pl.BlockSpec((None, None, MEM_WIDTH, N_MEM), lambda l, b: (l, b, 0, 0)),
            pl.BlockSpec((None, None, N_MEM, MEM_WIDTH), lambda l, b: (l, b, 0, 0)),
        ],
        out_shape=[
            jax.ShapeDtypeStruct((DEPTH, batch, MEM_WIDTH, N_MEM), BF16),
            jax.ShapeDtypeStruct((DEPTH, batch, N_MEM, MEM_WIDTH), BF16),
        ],
        compiler_params=_params("arbitrary", "arbitrary"),
        name="mem_kv",
    )(mem, g_mem.reshape(DEPTH, 1, D_MODEL), w_mem_kv, g_mk.reshape(DEPTH, 1, HEAD_DIM))


def _column_chunks(width):
    return [slice(c, c + MXU_COLS) for c in range(0, width, MXU_COLS)]


def _norm_row_chunks(x_ref, g_ref, h_ref):
    for r in range(0, x_ref.shape[0], NORM_CHUNK_ROWS):
        rows = slice(r, r + NORM_CHUNK_ROWS)
        x = x_ref[rows, :]
        h = (x * _rms_scale(x) * g_ref[...]).astype(BF16)
        h_ref[rows, :] = h
        yield rows, h


def _in_proj_gmlp_kernel(x_ref, g_ref, w_ref, z_ref, h_ref, *, gelu_tiles):
    j = pl.program_id(1)

    @pl.when(j == 0)
    def _():
        for rows, h in _norm_row_chunks(x_ref, g_ref, h_ref):
            z_ref[rows, :] = _gelu_exact(_dot(h, w_ref[...])).astype(BF16)

    @pl.when(jnp.logical_and(j > 0, j < gelu_tiles))
    def _():
        h = h_ref[...]
        for cols in _column_chunks(z_ref.shape[1]):
            z_ref[:, cols] = _gelu_exact(_dot(h, w_ref[:, cols])).astype(BF16)

    @pl.when(j >= gelu_tiles)
    def _():
        z_ref[...] = _dot(h_ref[...], w_ref[...]).astype(BF16)


def _in_proj_attn_kernel(x_ref, g_ref, w_ref, gain_ref, cos_ref, sin_ref, seg_ref, z_ref, h_ref, *, rope_tiles):
    j = pl.program_id(1)
    dot_cols = 2 * MXU_COLS

    def project_norm_rotate(h, rows):
        cos, sin = cos_ref[rows, :], sin_ref[rows, :]
        for c0 in range(0, z_ref.shape[1], dot_cols):
            acc = _dot(h, w_ref[:, c0:c0 + dot_cols])
            for p0 in range(0, dot_cols, 2 * HEAD_DIM):
                t0, t1 = acc[:, p0:p0 + HEAD_DIM], acc[:, p0 + HEAD_DIM:p0 + 2 * HEAD_DIM]
                lanes0 = slice(c0 + p0, c0 + p0 + HEAD_DIM)
                lanes1 = slice(c0 + p0 + HEAD_DIM, c0 + p0 + 2 * HEAD_DIM)
                ss = t0 * t0 + t1 * t1
                hi = ss.astype(BF16)
                lo = (ss - hi.astype(F32)).astype(BF16)
                tot = _dot(jnp.concatenate([hi, lo], axis=1), seg_ref[...])
                r = lax.rsqrt(tot * (1.0 / HEAD_DIM) + EPS)
                n0 = t0 * r * gain_ref[:, lanes0]
                n1 = t1 * r * gain_ref[:, lanes1]
                z_ref[rows, lanes0] = (n0 * cos - n1 * sin).astype(BF16)
                z_ref[rows, lanes1] = (n1 * cos + n0 * sin).astype(BF16)

    @pl.when(j == 0)
    def _():
        for rows, h in _norm_row_chunks(x_ref, g_ref, h_ref):
            project_norm_rotate(h, rows)

    @pl.when(jnp.logical_and(j > 0, j < rope_tiles))
    def _():
        project_norm_rotate(h_ref[...], slice(0, z_ref.shape[0]))

    @pl.when(j >= rope_tiles)
    def _():
        z_ref[...] = _dot(h_ref[...], w_ref[...]).astype(BF16)


def _in_proj(x2d, g, w, layer, *, attn=None):
    rows, _ = x2d.shape
    cols = w.shape[2]
    bm = IN_PROJ_ROWS
    bn = IN_PROJ_COLS_GMLP if attn is None else IN_PROJ_COLS_ATTN
    assert cols % bn == 0 and bn % MXU_COLS == 0
    grid = (rows // bm, cols // bn)
    in_specs = [
        pl.BlockSpec((bm, D_MODEL), lambda i, j: (i, 0)),
        pl.BlockSpec((1, D_MODEL), lambda i, j: (0, 0)),
        pl.BlockSpec((None, D_MODEL, bn), lambda i, j: (layer, 0, j)),
    ]
    args = [x2d, g.reshape(1, D_MODEL), w]
    if attn is None:
        assert (2 * TOK_WIDTH) % bn == 0
        body = functools.partial(_in_proj_gmlp_kernel, gelu_tiles=2 * TOK_WIDTH // bn)
    else:
        gain, cos, sin, seg = attn
        seq_tiles = cos.shape[0] // bm
        rope_width = TOK_WIDTH + KV_WIDTH
        assert rope_width % bn == 0
        body = functools.partial(_in_proj_attn_kernel, rope_tiles=rope_width // bn)
        in_specs += [
            pl.BlockSpec((1, bn), lambda i, j: (0, j)),
            pl.BlockSpec((bm, HEAD_DIM), lambda i, j: (i % seq_tiles, 0)),
            pl.BlockSpec((bm, HEAD_DIM), lambda i, j: (i % seq_tiles, 0)),
            pl.BlockSpec((2 * HEAD_DIM, HEAD_DIM), lambda i, j: (0, 0)),
        ]
        args += [gain, cos, sin, seg]
    return pl.pallas_call(
        body,
        grid=grid,
        in_specs=in_specs,
        out_specs=pl.BlockSpec((bm, bn), lambda i, j: (i, j)),
        out_shape=jax.ShapeDtypeStruct((rows, cols), BF16),
        scratch_shapes=[pltpu.VMEM((bm, D_MODEL), BF16)],
        compiler_params=_params("parallel", "arbitrary"),
        name="in_proj_gmlp" if attn is None else "in_proj_attn",
    )(*args)


_NT_DIMS = (((1,), (1,)), ((), ()))


def _attention_kernel(q_ref, k_ref, v_ref, o_ref):
    k = k_ref[...]
    v = v_ref[...]
    half = HEAD_DIM // 2
    lane = lax.broadcasted_iota(jnp.int32, (ATTN_CHAIN_ROWS, 2 * HEAD_DIM), 1) % HEAD_DIM
    first_lane = (pl.program_id(1) % 2) * half
    keep = jnp.logical_and(lane >= first_lane, lane < first_lane + half)
    for r in range(0, q_ref.shape[0], ATTN_CHAIN_ROWS):
        rows = slice(r, r + ATTN_CHAIN_ROWS)
        for g in range(Q_PER_KV):
            q = q_ref[rows, g * 2 * HEAD_DIM:(g + 1) * 2 * HEAD_DIM]
            q = jnp.where(keep, q, jnp.zeros_like(q))
            s = lax.dot_general(q, k, _NT_DIMS, preferred_element_type=F32)
            p = jnp.exp2(s - jnp.max(s, axis=-1, keepdims=True))
            denom = jnp.sum(p, axis=-1, keepdims=True)
            o = _dot(p.astype(BF16), v) / denom
            o_ref[rows, g * HEAD_DIM:(g + 1) * HEAD_DIM] = o.astype(BF16)


def _attention(z, batch, seq):
    bq = ATTN_Q_ROWS
    q_tiles = seq // bq
    pair_width = 2 * HEAD_DIM
    k_block0 = TOK_WIDTH // pair_width
    v_block0 = (TOK_WIDTH + KV_WIDTH) // HEAD_DIM
    return pl.pallas_call(
        _attention_kernel,
        grid=(batch, KV_HEADS, q_tiles),
        in_specs=[
            pl.BlockSpec((bq, Q_PER_KV * pair_width), lambda b, h, i: (b * q_tiles + i, h // 2)),
            pl.BlockSpec((seq, pair_width), lambda b, h, i: (b, k_block0 + h // 2)),
            pl.BlockSpec((seq, HEAD_DIM), lambda b, h, i: (b, v_block0 + h)),
        ],
        out_specs=pl.BlockSpec((bq, Q_PER_KV * HEAD_DIM), lambda b, h, i: (b * q_tiles + i, h)),
        out_shape=jax.ShapeDtypeStruct((batch * seq, TOK_WIDTH), BF16),
        compiler_params=_params("parallel", "parallel", "arbitrary"),
        name="attention",
    )(z, z, z)


def _memory_attention_into(mixed_ref, qm_ref, g_mq_ref, kt_ref, v_ref):
    for hd in range(MEM_HEADS):
        lanes = slice(hd * HEAD_DIM, (hd + 1) * HEAD_DIM)
        q = qm_ref[:, lanes].astype(F32)
        q = (q * _rms_scale(q) * g_mq_ref[...] * SCORE_SCALE).astype(BF16)
        s = _dot(q, kt_ref[lanes, :])
        p = jnp.exp(s - jnp.max(s, axis=-1, keepdims=True))
        denom = jnp.sum(p, axis=-1, keepdims=True)
        o = _dot(p.astype(BF16), v_ref[:, lanes]) / denom
        mixed_ref[:, TOK_WIDTH + hd * HEAD_DIM:TOK_WIDTH + (hd + 1) * HEAD_DIM] = o.astype(BF16)


def _mix_out_gmlp_kernel(u_ref, v_ref, qm_ref, g_v_ref, ws_ref, bs_ref, g_mq_ref, kt_ref, mv_ref,
                         w_out_ref, x_ref, o_ref, mixed_ref):
    v = v_ref[...].astype(F32)
    vn = (v * _rms_scale(v) * g_v_ref[...]).astype(BF16)
    for c in range(v.shape[0] // CHUNK):
        rows = slice(c * CHUNK, (c + 1) * CHUNK)
        for g in range(A_GROUPS):
            lanes = slice(g * HEAD_DIM, (g + 1) * HEAD_DIM)
            s = _dot(ws_ref[g], vn[rows, lanes]) + bs_ref[g]
            mixed_ref[rows, lanes] = (u_ref[rows, lanes].astype(F32) * s).astype(BF16)
    _memory_attention_into(mixed_ref, qm_ref, g_mq_ref, kt_ref, mv_ref)
    o_ref[...] = x_ref[...] + _dot(mixed_ref[...], w_out_ref[...])


def _mix_out_attn_kernel(t_ref, qm_ref, g_mq_ref, kt_ref, mv_ref, w_out_ref, x_ref, o_ref, mixed_ref):
    mixed_ref[:, :TOK_WIDTH] = t_ref[...]
    _memory_attention_into(mixed_ref, qm_ref, g_mq_ref, kt_ref, mv_ref)
    o_ref[...] = x_ref[...] + _dot(mixed_ref[...], w_out_ref[...])


def _mix_out(x2d, z, mem_kt, mem_v, g_mq, w_out, layer, seq, *, gmlp=None, tok=None):
    rows = x2d.shape[0]
    bm = MIX_ROWS
    seq_tiles = seq // bm
    qm_col = (z.shape[1] - MEM_WIDTH) // MEM_WIDTH
    shared_specs = [
        pl.BlockSpec((bm, MEM_WIDTH), lambda i: (i, qm_col)),
    ]
    tail_specs = [
        pl.BlockSpec((1, HEAD_DIM), lambda i: (0, 0)),
        pl.BlockSpec((None, None, MEM_WIDTH, N_MEM), lambda i: (layer, i // seq_tiles, 0, 0)),
        pl.BlockSpec((None, None, N_MEM, MEM_WIDTH), lambda i: (layer, i // seq_tiles, 0, 0)),
        pl.BlockSpec((None, D_MODEL, D_MODEL), lambda i: (layer, 0, 0), pipeline_mode=pl.Buffered(1)),
        pl.BlockSpec((bm, D_MODEL), lambda i: (i, 0)),
    ]
    tail_args = [g_mq.reshape(1, HEAD_DIM), mem_kt, mem_v, w_out, x2d]
    if gmlp is not None:
        g_v, w_s, b_s, gmlp_layer = gmlp
        body = _mix_out_gmlp_kernel
        in_specs = [
            pl.BlockSpec((bm, TOK_WIDTH), lambda i: (i, 0)),
            pl.BlockSpec((bm, TOK_WIDTH), lambda i: (i, 1)),
        ] + shared_specs + [
            pl.BlockSpec((1, TOK_WIDTH), lambda i: (0, 0)),
            pl.BlockSpec((None, A_GROUPS, CHUNK, CHUNK), lambda i: (gmlp_layer, 0, 0, 0)),
            pl.BlockSpec((A_GROUPS, CHUNK, 1), lambda i: (0, 0, 0)),
        ] + tail_specs
        args = [z, z, z, g_v.reshape(1, TOK_WIDTH), w_s, b_s.reshape(A_GROUPS, CHUNK, 1)] + tail_args
        name = "mix_out_gmlp"
    else:
        body = _mix_out_attn_kernel
        in_specs = [pl.BlockSpec((bm, TOK_WIDTH), lambda i: (i, 0))] + shared_specs + tail_specs
        args = [tok, z] + tail_args
        name = "mix_out_attn"
    return pl.pallas_call(
        body,
        grid=(rows // bm,),
        in_specs=in_specs,
        out_specs=pl.BlockSpec((bm, D_MODEL), lambda i: (i, 0)),
        out_shape=jax.ShapeDtypeStruct((rows, D_MODEL), F32),
        scratch_shapes=[pltpu.VMEM((bm, D_MODEL), BF16)],
        compiler_params=_params("parallel"),
        name=name,
    )(*args)


def _ffn_kernel(x_ref, g_ref, wg_ref, wu_ref, wd_ref, o_ref, h_ref):
    j = pl.program_id(1)

    def contribution(h):
        gate = _dot(h, wg_ref[...])
        up = _dot(h, wu_ref[...])
        act = (jax.nn.silu(gate) * up).astype(BF16)
        return _dot(act, wd_ref[...])

    @pl.when(j == 0)
    def _():
        for rows, h in _norm_row_chunks(x_ref, g_ref, h_ref):
            o_ref[rows, :] = x_ref[rows, :] + contribution(h)

    @pl.when(j > 0)
    def _():
        o_ref[...] += contribution(h_ref[...])


def _ffn(x2d, g, w_gate_up, w_down, layer):
    rows = x2d.shape[0]
    d_ff = w_down.shape[1]
    bm, bf = FFN_ROWS, FFN_COLS
    ff_tiles = d_ff // bf
    return pl.pallas_call(
        _ffn_kernel,
        grid=(rows // bm, ff_tiles),
        in_specs=[
            pl.BlockSpec((bm, D_MODEL), lambda i, j: (i, 0)),
            pl.BlockSpec((1, D_MODEL), lambda i, j: (0, 0)),
            pl.BlockSpec((None, D_MODEL, bf), lambda i, j: (layer, 0, j)),
            pl.BlockSpec((None, D_MODEL, bf), lambda i, j: (layer, 0, ff_tiles + j)),
            pl.BlockSpec((None, bf, D_MODEL), lambda i, j: (layer, j, 0)),
        ],
        out_specs=pl.BlockSpec((bm, D_MODEL), lambda i, j: (i, 0)),
        out_shape=jax.ShapeDtypeStruct((rows, D_MODEL), F32),
        scratch_shapes=[pltpu.VMEM((bm, D_MODEL), BF16)],
        compiler_params=_params("parallel", "arbitrary"),
        name="ffn",
    )(x2d, g.reshape(1, D_MODEL), w_gate_up, w_gate_up, w_down)


def _pair_layout_columns(a, kv_groups):
    lead = a.shape[:-1]
    n = len(lead)
    a = a.reshape(lead + (KV_HEADS // 2, 2, kv_groups, 2, 2, ROPE_PAIRS))
    a = a.transpose(tuple(range(n)) + (n, n + 2, n + 4, n + 1, n + 3, n + 5))
    return a.reshape(lead + (KV_HEADS * kv_groups * HEAD_DIM,))


def _pair_layout_weights(w):
    q, k, rest = w[..., :TOK_WIDTH], w[..., TOK_WIDTH:TOK_WIDTH + KV_WIDTH], w[..., TOK_WIDTH + KV_WIDTH:]
    return jnp.concatenate([_pair_layout_columns(q, Q_PER_KV), _pair_layout_columns(k, 1), rest], axis=-1)


def _attn_tables(seq):
    n_rows = seq // GRID_W
    rows = jnp.broadcast_to(jnp.arange(n_rows)[:, None], (n_rows, GRID_W)).reshape(seq)
    cols = jnp.broadcast_to(jnp.arange(GRID_W)[None, :], (n_rows, GRID_W)).reshape(seq)
    freqs = ROPE_THETA ** (-jnp.arange(ROPE_PAIRS, dtype=F32) / ROPE_PAIRS)
    ang_r = rows.astype(F32)[:, None] * freqs
    ang_c = cols.astype(F32)[:, None] * freqs
    ang = jnp.concatenate([ang_r, ang_c, ang_r, ang_c], axis=-1)
    k_seg = (jnp.arange(2 * HEAD_DIM) % HEAD_DIM) // (HEAD_DIM // 2)
    n_seg = jnp.arange(HEAD_DIM) // (HEAD_DIM // 2)
    seg = (k_seg[:, None] == n_seg[None, :]).astype(BF16)
    return jnp.cos(ang), jnp.sin(ang), seg


def _attn_gains(g_q, g_k, cols):
    gq = _pair_layout_columns(jnp.tile(g_q * (SCORE_SCALE * LOG2_E), Q_HEADS), Q_PER_KV)
    gk = _pair_layout_columns(jnp.tile(g_k, KV_HEADS), 1)
    return jnp.concatenate([gq, gk, jnp.ones((cols - TOK_WIDTH - KV_WIDTH,), F32)]).reshape(1, cols)


def kernel(x, mem, g_mix, g_ffn, w_in_a, g_v_a, w_spatial, b_spatial, w_in_b, g_q_b, g_k_b, g_mem, w_mem_kv,
           g_mq, g_mk, w_out, w_gate_up, w_down):
    batch, seq, _ = x.shape
    rows = batch * seq
    assert seq % IN_PROJ_ROWS == 0 and seq % MIX_ROWS == 0 and seq % ATTN_Q_ROWS == 0 and rows % FFN_ROWS == 0
    assert MIX_ROWS % CHUNK == 0 and w_down.shape[1] % FFN_COLS == 0

    cos, sin, seg = _attn_tables(seq)
    mem_kt, mem_v = _mem_kv(mem, g_mem, w_mem_kv.astype(BF16), g_mk)
    w_in_a, w_spatial, w_out = w_in_a.astype(BF16), w_spatial.astype(BF16), w_out.astype(BF16)
    w_in_b = _pair_layout_weights(w_in_b.astype(BF16))
    w_gate_up, w_down = w_gate_up.astype(BF16), w_down.astype(BF16)
    xs = x.reshape(rows, D_MODEL)
    for l in range(DEPTH):
        idx = l // 2
        if l % 2 == 0:
            z = _in_proj(xs, g_mix[l], w_in_a, idx)
            xs = _mix_out(xs, z, mem_kt, mem_v, g_mq[l], w_out, l, seq,
                          gmlp=(g_v_a[idx], w_spatial, b_spatial[idx], idx))
        else:
            gain = _attn_gains(g_q_b[idx], g_k_b[idx], w_in_b.shape[2])
            z = _in_proj(xs, g_mix[l], w_in_b, idx, attn=(gain, cos, sin, seg))
            tok = _attention(z, batch, seq)
            xs = _mix_out(xs, z, mem_kt, mem_v, g_mq[l], w_out, l, seq, tok=tok)
        xs = _ffn(xs, g_ffn[l], w_gate_up, w_down, l)
    return xs.reshape(batch, seq, D_MODEL)
```

```python
import functools

import jax
import jax.numpy as jnp
from jax import lax
from jax.experimental import pallas as pl
from jax.experimental.pallas import tpu as pltpu

D_MODEL = 2048
DEPTH = 4
N_MEM = 256
GRID_W = 64
HEAD_DIM = 128
MEM_HEADS = 4
MEM_WIDTH = MEM_HEADS * HEAD_DIM
TOK_WIDTH = D_MODEL - MEM_WIDTH
CHUNK = 128
A_GROUPS = TOK_WIDTH // HEAD_DIM
Q_HEADS = TOK_WIDTH // HEAD_DIM
KV_HEADS = 4
Q_PER_KV = Q_HEADS // KV_HEADS
KV_WIDTH = KV_HEADS * HEAD_DIM
ROPE_THETA = 10000.0
ROPE_PAIRS = HEAD_DIM // 4
EPS = 1e-6
SCORE_SCALE = HEAD_DIM ** -0.5

VMEM_LIMIT_BYTES = 63 * 1024 * 1024
MXU_COLS = 256
LOG2_E = 1.4426950408889634

IN_PROJ_ROWS = 1024
IN_PROJ_DOT_COLS = 512
MIX_ROWS = 512
ATTN_Q_ROWS = 1024
ATTN_CHAIN_ROWS = 256
FFN_ROWS = 1024
FFN_COLS = 512
NORM_CHUNK_ROWS = 256

BF16 = jnp.bfloat16
F32 = jnp.float32


def _rms_scale(x):
    return lax.rsqrt(jnp.mean(x * x, axis=-1, keepdims=True) + EPS)


def _gelu_exact(x):
    return 0.5 * x * (1.0 + lax.erf(x * (2.0 ** -0.5)))


def _dot(a, b):
    return jnp.dot(a, b, preferred_element_type=F32)


def _params(*semantics):
    return pltpu.CompilerParams(dimension_semantics=semantics, vmem_limit_bytes=VMEM_LIMIT_BYTES)


def _mem_kv_kernel(mem_ref, g_mem_ref, w_ref, g_mk_ref, kt_ref, v_ref):
    m = mem_ref[...]
    h = (m * _rms_scale(m) * g_mem_ref[...]).astype(BF16)
    kv = _dot(h, w_ref[...])
    for hd in range(MEM_HEADS):
        k = kv[:, hd * HEAD_DIM:(hd + 1) * HEAD_DIM]
        k = k * _rms_scale(k) * g_mk_ref[...]
        kt_ref[hd * HEAD_DIM:(hd + 1) * HEAD_DIM, :] = k.T.astype(BF16)
    v_ref[...] = kv[:, MEM_WIDTH:].astype(BF16)


def _mem_kv(mem, g_mem, w_mem_kv, g_mk):
    batch = mem.shape[0]
    return pl.pallas_call(
        _mem_kv_kernel,
        grid=(DEPTH, batch),
        in_specs=[
            pl.BlockSpec((None, N_MEM, D_MODEL), lambda l, b: (b, 0, 0)),
            pl.BlockSpec((None, 1, D_MODEL), lambda l, b: (l, 0, 0)),
            pl.BlockSpec((None, D_MODEL, 2 * MEM_WIDTH), lambda l, b: (l, 0, 0)),
            pl.BlockSpec((None, 1, HEAD_DIM), lambda l, b: (l, 0, 0)),
        ],
        out_specs=[
            pl.BlockSpec((None, None, MEM_WIDTH, N_MEM), lambda l, b: (l, b, 0, 0)),
            pl.BlockSpec((None, None, N_MEM, MEM_WIDTH), lambda l, b: (l, b, 0, 0)),
        ],
        out_shape=[
            jax.ShapeDtypeStruct((DEPTH, batch, MEM_WIDTH, N_MEM), BF16),
            jax.ShapeDtypeStruct((DEPTH, batch, N_MEM, MEM_WIDTH), BF16),
        ],
        compiler_params=_params("arbitrary", "arbitrary"),
        name="mem_kv",
    )(mem, g_mem.reshape(DEPTH, 1, D_MODEL), w_mem_kv, g_mk.reshape(DEPTH, 1, HEAD_DIM))


def _norm_row_chunks(x_ref, g_ref, h_ref):
    for r in range(0, x_ref.shape[0], NORM_CHUNK_ROWS):
        rows = slice(r, r + NORM_CHUNK_ROWS)
        x = x_ref[rows, :]
        h = (x * _rms_scale(x) * g_ref[...]).astype(BF16)
        h_ref[rows, :] = h
        yield rows, h


def _project_column_chunks(x_ref, g_ref, h_ref, n_cols, project):
    for rows, h in _norm_row_chunks(x_ref, g_ref, h_ref):
        project(h, rows, 0)
    h = h_ref[...]
    for c0 in range(IN_PROJ_DOT_COLS, n_cols, IN_PROJ_DOT_COLS):
        project(h, slice(0, x_ref.shape[0]), c0)


def _in_proj_gmlp_kernel(x_ref, g_ref, w_ref, z_ref, h_ref, *, gelu_cols):
    def project(h, rows, c0):
        cols = slice(c0, c0 + IN_PROJ_DOT_COLS)
        acc = _dot(h, w_ref[:, cols])
        z_ref[rows, cols] = (_gelu_exact(acc) if c0 < gelu_cols else acc).astype(BF16)

    _project_column_chunks(x_ref, g_ref, h_ref, z_ref.shape[1], project)


def _in_proj_attn_kernel(x_ref, g_ref, w_ref, gain_ref, cos_ref, sin_ref, seg_ref, z_ref, h_ref, *, rope_cols):
    def project(h, rows, c0):
        acc = _dot(h, w_ref[:, c0:c0 + IN_PROJ_DOT_COLS])
        if c0 >= rope_cols:
            z_ref[rows, c0:c0 + IN_PROJ_DOT_COLS] = acc.astype(BF16)
            return
        cos, sin = cos_ref[rows, :], sin_ref[rows, :]
        for p0 in range(0, IN_PROJ_DOT_COLS, 2 * HEAD_DIM):
            t0, t1 = acc[:, p0:p0 + HEAD_DIM], acc[:, p0 + HEAD_DIM:p0 + 2 * HEAD_DIM]
            lanes0 = slice(c0 + p0, c0 + p0 + HEAD_DIM)
            lanes1 = slice(c0 + p0 + HEAD_DIM, c0 + p0 + 2 * HEAD_DIM)
            ss = t0 * t0 + t1 * t1
            hi = ss.astype(BF16)
            lo = (ss - hi.astype(F32)).astype(BF16)
            tot = _dot(jnp.concatenate([hi, lo], axis=1), seg_ref[...])
            r = lax.rsqrt(tot * (1.0 / HEAD_DIM) + EPS)
            n0 = t0 * r * gain_ref[:, lanes0]
            n1 = t1 * r * gain_ref[:, lanes1]
            z_ref[rows, lanes0] = (n0 * cos - n1 * sin).astype(BF16)
            z_ref[rows, lanes1] = (n1 * cos + n0 * sin).astype(BF16)

    _project_column_chunks(x_ref, g_ref, h_ref, z_ref.shape[1], project)


def _in_proj(x2d, g, w, layer, *, attn=None):
    rows, _ = x2d.shape
    cols = w.shape[2]
    bm = IN_PROJ_ROWS
    assert cols % IN_PROJ_DOT_COLS == 0
    resident = pl.Buffered(1)
    in_specs = [
        pl.BlockSpec((bm, D_MODEL), lambda i: (i, 0)),
        pl.BlockSpec((1, D_MODEL), lambda i: (0, 0)),
        pl.BlockSpec((None, D_MODEL, cols), lambda i: (layer, 0, 0), pipeline_mode=resident),
    ]
    args = [x2d, g.reshape(1, D_MODEL), w]
    if attn is None:
        assert (2 * TOK_WIDTH) % IN_PROJ_DOT_COLS == 0
        body = functools.partial(_in_proj_gmlp_kernel, gelu_cols=2 * TOK_WIDTH)
    else:
        gain, cos, sin, seg = attn
        seq_tiles = cos.shape[0] // bm
        assert (TOK_WIDTH + KV_WIDTH) % IN_PROJ_DOT_COLS == 0
        body = functools.partial(_in_proj_attn_kernel, rope_cols=TOK_WIDTH + KV_WIDTH)
        in_specs += [
            pl.BlockSpec((1, cols), lambda i: (0, 0)),
            pl.BlockSpec((bm, HEAD_DIM), lambda i: (i % seq_tiles, 0)),
            pl.BlockSpec((bm, HEAD_DIM), lambda i: (i % seq_tiles, 0)),
            pl.BlockSpec((2 * HEAD_DIM, HEAD_DIM), lambda i: (0, 0)),
        ]
        args += [gain, cos, sin, seg]
    return pl.pallas_call(
        body,
        grid=(rows // bm,),
        in_specs=in_specs,
        out_specs=pl.BlockSpec((bm, cols), lambda i: (i, 0)),
        out_shape=jax.ShapeDtypeStruct((rows, cols), BF16),
        scratch_shapes=[pltpu.VMEM((bm, D_MODEL), BF16)],
        compiler_params=_params("parallel"),
        name="in_proj_gmlp" if attn is None else "in_proj_attn",
    )(*args)


_NT_DIMS = (((1,), (1,)), ((), ()))


def _attention_kernel(q_ref, k_ref, v_ref, o_ref):
    k = k_ref[...]
    v = v_ref[...]
    half = HEAD_DIM // 2
    lane = lax.broadcasted_iota(jnp.int32, (ATTN_CHAIN_ROWS, 2 * HEAD_DIM), 1) % HEAD_DIM
    first_lane = (pl.program_id(1) % 2) * half
    keep = jnp.logical_and(lane >= first_lane, lane < first_lane + half)
    for r in range(0, q_ref.shape[0], ATTN_CHAIN_ROWS):
        rows = slice(r, r + ATTN_CHAIN_ROWS)
        for g in range(Q_PER_KV):
            q = q_ref[rows, g * 2 * HEAD_DIM:(g + 1) * 2 * HEAD_DIM]
            q = jnp.where(keep, q, jnp.zeros_like(q))
            s = lax.dot_general(q, k, _NT_DIMS, preferred_element_type=F32)
            p = jnp.exp2(s - jnp.max(s, axis=-1, keepdims=True))
            denom = jnp.sum(p, axis=-1, keepdims=True)
            o = _dot(p.astype(BF16), v) / denom
            o_ref[rows, g * HEAD_DIM:(g + 1) * HEAD_DIM] = o.astype(BF16)


def _attention(z, batch, seq):
    bq = ATTN_Q_ROWS
    q_tiles = seq // bq
    pair_width = 2 * HEAD_DIM
    k_block0 = TOK_WIDTH // pair_width
    v_block0 = (TOK_WIDTH + KV_WIDTH) // HEAD_DIM
    return pl.pallas_call(
        _attention_kernel,
        grid=(batch, KV_HEADS, q_tiles),
        in_specs=[
            pl.BlockSpec((bq, Q_PER_KV * pair_width), lambda b, h, i: (b * q_tiles + i, h // 2)),
            pl.BlockSpec((seq, pair_width), lambda b, h, i: (b, k_block0 + h // 2)),
            pl.BlockSpec((seq, HEAD_DIM), lambda b, h, i: (b, v_block0 + h)),
        ],
        out_specs=pl.BlockSpec((bq, Q_PER_KV * HEAD_DIM), lambda b, h, i: (b * q_tiles + i, h)),
        out_shape=jax.ShapeDtypeStruct((batch * seq, TOK_WIDTH), BF16),
        compiler_params=_params("parallel", "parallel", "arbitrary"),
        name="attention",
    )(z, z, z)


def _memory_attention_into(mixed_ref, qm_ref, g_mq_ref, kt_ref, v_ref):
    for hd in range(MEM_HEADS):
        lanes = slice(hd * HEAD_DIM, (hd + 1) * HEAD_DIM)
        q = qm_ref[:, lanes].astype(F32)
        q = (q * _rms_scale(q) * g_mq_ref[...] * SCORE_SCALE).astype(BF16)
        s = _dot(q, kt_ref[lanes, :])
        p = jnp.exp(s - jnp.max(s, axis=-1, keepdims=True))
        denom = jnp.sum(p, axis=-1, keepdims=True)
        o = _dot(p.astype(BF16), v_ref[:, lanes]) / denom
        mixed_ref[:, TOK_WIDTH + hd * HEAD_DIM:TOK_WIDTH + (hd + 1) * HEAD_DIM] = o.astype(BF16)


def _mix_out_gmlp_kernel(u_ref, v_ref, qm_ref, g_v_ref, ws_ref, bs_ref, g_mq_ref, kt_ref, mv_ref,
                         w_out_ref, x_ref, o_ref, mixed_ref):
    v = v_ref[...].astype(F32)
    vn = (v * _rms_scale(v) * g_v_ref[...]).astype(BF16)
    for c in range(v.shape[0] // CHUNK):
        rows = slice(c * CHUNK, (c + 1) * CHUNK)
        for g in range(A_GROUPS):
            lanes = slice(g * HEAD_DIM, (g + 1) * HEAD_DIM)
            s = _dot(ws_ref[g], vn[rows, lanes]) + bs_ref[g]
            mixed_ref[rows, lanes] = (u_ref[rows, lanes].astype(F32) * s).astype(BF16)
    _memory_attention_into(mixed_ref, qm_ref, g_mq_ref, kt_ref, mv_ref)
    o_ref[...] = x_ref[...] + _dot(mixed_ref[...], w_out_ref[...])


def _mix_out_attn_kernel(t_ref, qm_ref, g_mq_ref, kt_ref, mv_ref, w_out_ref, x_ref, o_ref, mixed_ref):
    mixed_ref[:, :TOK_WIDTH] = t_ref[...]
    _memory_attention_into(mixed_ref, qm_ref, g_mq_ref, kt_ref, mv_ref)
    o_ref[...] = x_ref[...] + _dot(mixed_ref[...], w_out_ref[...])


def _mix_out(x2d, z, mem_kt, mem_v, g_mq, w_out, layer, seq, *, gmlp=None, tok=None):
    rows = x2d.shape[0]
    bm = MIX_ROWS
    seq_tiles = seq // bm
    qm_col = (z.shape[1] - MEM_WIDTH) // MEM_WIDTH
    shared_specs = [
        pl.BlockSpec((bm, MEM_WIDTH), lambda i: (i, qm_col)),
    ]
    tail_specs = [
        pl.BlockSpec((1, HEAD_DIM), lambda i: (0, 0)),
        pl.BlockSpec((None, None, MEM_WIDTH, N_MEM), lambda i: (layer, i // seq_tiles, 0, 0)),
        pl.BlockSpec((None, None, N_MEM, MEM_WIDTH), lambda i: (layer, i // seq_tiles, 0, 0)),
        pl.BlockSpec((None, D_MODEL, D_MODEL), lambda i: (layer, 0, 0), pipeline_mode=pl.Buffered(1)),
        pl.BlockSpec((bm, D_MODEL), lambda i: (i, 0)),
    ]
    tail_args = [g_mq.reshape(1, HEAD_DIM), mem_kt, mem_v, w_out, x2d]
    if gmlp is not None:
        g_v, w_s, b_s, gmlp_layer = gmlp
        body = _mix_out_gmlp_kernel
        in_specs = [
            pl.BlockSpec((bm, TOK_WIDTH), lambda i: (i, 0)),
            pl.BlockSpec((bm, TOK_WIDTH), lambda i: (i, 1)),
        ] + shared_specs + [
            pl.BlockSpec((1, TOK_WIDTH), lambda i: (0, 0)),
            pl.BlockSpec((None, A_GROUPS, CHUNK, CHUNK), lambda i: (gmlp_layer, 0, 0, 0)),
            pl.BlockSpec((A_GROUPS, CHUNK, 1), lambda i: (0, 0, 0)),
        ] + tail_specs
        args = [z, z, z, g_v.reshape(1, TOK_WIDTH), w_s, b_s.reshape(A_GROUPS, CHUNK, 1)] + tail_args
        name = "mix_out_gmlp"
    else:
        body = _mix_out_attn_kernel
        in_specs = [pl.BlockSpec((bm, TOK_WIDTH), lambda i: (i, 0))] + shared_specs + tail_specs
        args = [tok, z] + tail_args
        name = "mix_out_attn"
    return pl.pallas_call(
        body,
        grid=(rows // bm,),
        in_specs=in_specs,
        out_specs=pl.BlockSpec((bm, D_MODEL), lambda i: (i, 0)),
        out_shape=jax.ShapeDtypeStruct((rows, D_MODEL), F32),
        scratch_shapes=[pltpu.VMEM((bm, D_MODEL), BF16)],
        compiler_params=_params("parallel"),
        name=name,
    )(*args)


def _ffn_kernel(x_ref, g_ref, wg_ref, wu_ref, wd_ref, o_ref, h_ref):
    j = pl.program_id(1)

    def contribution(h):
        gate = _dot(h, wg_ref[...])
        up = _dot(h, wu_ref[...])
        act = (jax.nn.silu(gate) * up).astype(BF16)
        return _dot(act, wd_ref[...])

    @pl.when(j == 0)
    def _():
        for rows, h in _norm_row_chunks(x_ref, g_ref, h_ref):
            o_ref[rows, :] = x_ref[rows, :] + contribution(h)

    @pl.when(j > 0)
    def _():
        o_ref[...] += contribution(h_ref[...])


def _ffn(x2d, g, w_gate_up, w_down, layer):
    rows = x2d.shape[0]
    d_ff = w_down.shape[1]
    bm, bf = FFN_ROWS, FFN_COLS
    ff_tiles = d_ff // bf
    return pl.pallas_call(
        _ffn_kernel,
        grid=(rows // bm, ff_tiles),
        in_specs=[
            pl.BlockSpec((bm, D_MODEL), lambda i, j: (i, 0)),
            pl.BlockSpec((1, D_MODEL), lambda i, j: (0, 0)),
            pl.BlockSpec((None, D_MODEL, bf), lambda i, j: (layer, 0, j)),
            pl.BlockSpec((None, D_MODEL, bf), lambda i, j: (layer, 0, ff_tiles + j)),
            pl.BlockSpec((None, bf, D_MODEL), lambda i, j: (layer, j, 0)),
        ],
        out_specs=pl.BlockSpec((bm, D_MODEL), lambda i, j: (i, 0)),
        out_shape=jax.ShapeDtypeStruct((rows, D_MODEL), F32),
        scratch_shapes=[pltpu.VMEM((bm, D_MODEL), BF16)],
        compiler_params=_params("parallel", "arbitrary"),
        name="ffn",
    )(x2d, g.reshape(1, D_MODEL), w_gate_up, w_gate_up, w_down)


def _pair_layout_columns(a, kv_groups):
    lead = a.shape[:-1]
    n = len(lead)
    a = a.reshape(lead + (KV_HEADS // 2, 2, kv_groups, 2, 2, ROPE_PAIRS))
    a = a.transpose(tuple(range(n)) + (n, n + 2, n + 4, n + 1, n + 3, n + 5))
    return a.reshape(lead + (KV_HEADS * kv_groups * HEAD_DIM,))


def _pair_layout_weights(w):
    q, k, rest = w[..., :TOK_WIDTH], w[..., TOK_WIDTH:TOK_WIDTH + KV_WIDTH], w[..., TOK_WIDTH + KV_WIDTH:]
    return jnp.concatenate([_pair_layout_columns(q, Q_PER_KV), _pair_layout_columns(k, 1), rest], axis=-1)


def _attn_tables(seq):
    n_rows = seq // GRID_W
    rows = jnp.broadcast_to(jnp.arange(n_rows)[:, None], (n_rows, GRID_W)).reshape(seq)
    cols = jnp.broadcast_to(jnp.arange(GRID_W)[None, :], (n_rows, GRID_W)).reshape(seq)
    freqs = ROPE_THETA ** (-jnp.arange(ROPE_PAIRS, dtype=F32) / ROPE_PAIRS)
    ang_r = rows.astype(F32)[:, None] * freqs
    ang_c = cols.astype(F32)[:, None] * freqs
    ang = jnp.concatenate([ang_r, ang_c, ang_r, ang_c], axis=-1)
    k_seg = (jnp.arange(2 * HEAD_DIM) % HEAD_DIM) // (HEAD_DIM // 2)
    n_seg = jnp.arange(HEAD_DIM) // (HEAD_DIM // 2)
    seg = (k_seg[:, None] == n_seg[None, :]).astype(BF16)
    return jnp.cos(ang), jnp.sin(ang), seg


def _attn_gains(g_q, g_k, cols):
    gq = _pair_layout_columns(jnp.tile(g_q * (SCORE_SCALE * LOG2_E), Q_HEADS), Q_PER_KV)
    gk = _pair_layout_columns(jnp.tile(g_k, KV_HEADS), 1)
    return jnp.concatenate([gq, gk, jnp.ones((cols - TOK_WIDTH - KV_WIDTH,), F32)]).reshape(1, cols)


def kernel(x, mem, g_mix, g_ffn, w_in_a, g_v_a, w_spatial, b_spatial, w_in_b, g_q_b, g_k_b, g_mem, w_mem_kv,
           g_mq, g_mk, w_out, w_gate_up, w_down):
    batch, seq, _ = x.shape
    rows = batch * seq
    assert seq % IN_PROJ_ROWS == 0 and seq % MIX_ROWS == 0 and seq % ATTN_Q_ROWS == 0 and rows % FFN_ROWS == 0
    assert MIX_ROWS % CHUNK == 0 and w_down.shape[1] % FFN_COLS == 0

    cos, sin, seg = _attn_tables(seq)
    mem_kt, mem_v = _mem_kv(mem, g_mem, w_mem_kv.astype(BF16), g_mk)
    w_in_a, w_spatial, w_out = w_in_a.astype(BF16), w_spatial.astype(BF16), w_out.astype(BF16)
    w_in_b = _pair_layout_weights(w_in_b.astype(BF16))
    w_gate_up, w_down = w_gate_up.astype(BF16), w_down.astype(BF16)
    xs = x.reshape(rows, D_MODEL)
    for l in range(DEPTH):
        idx = l // 2
        if l % 2 == 0:
            z = _in_proj(xs, g_mix[l], w_in_a, idx)
            xs = _mix_out(xs, z, mem_kt, mem_v, g_mq[l], w_out, l, seq,
                          gmlp=(g_v_a[idx], w_spatial, b_spatial[idx], idx))
        else:
            gain = _attn_gains(g_q_b[idx], g_k_b[idx], w_in_b.shape[2])
            z = _in_proj(xs, g_mix[l], w_in_b, idx, attn=(gain, cos, sin, seg))
            tok = _attention(z, batch, seq)
            xs = _mix_out(xs, z, mem_kt, mem_v, g_mq[l], w_out, l, seq, tok=tok)
        xs = _ffn(xs, g_ffn[l], w_gate_up, w_down, l)
    return xs.reshape(batch, seq, D_MODEL)
```

```python
import functools

import jax
import jax.numpy as jnp
from jax import lax
from jax.experimental import pallas as pl
from jax.experimental.pallas import tpu as pltpu

D_MODEL = 2048
DEPTH = 4
N_MEM = 256
GRID_W = 64
HEAD_DIM = 128
MEM_HEADS = 4
MEM_WIDTH = MEM_HEADS * HEAD_DIM
TOK_WIDTH = D_MODEL - MEM_WIDTH
CHUNK = 128
A_GROUPS = TOK_WIDTH // HEAD_DIM
Q_HEADS = TOK_WIDTH // HEAD_DIM
KV_HEADS = 4
Q_PER_KV = Q_HEADS // KV_HEADS
KV_WIDTH = KV_HEADS * HEAD_DIM
ROPE_THETA = 10000.0
ROPE_PAIRS = HEAD_DIM // 4
EPS = 1e-6
SCORE_SCALE = HEAD_DIM ** -0.5

VMEM_LIMIT_BYTES = 63 * 1024 * 1024
MXU_COLS = 256
LOG2_E = 1.4426950408889634

IN_PROJ_ROWS = 1024
IN_PROJ_DOT_COLS = 512
MIX_ROWS_GMLP = 512
MIX_ROWS_ATTN = 1024
ATTN_Q_ROWS = 1024
ATTN_CHAIN_ROWS = 256
FFN_ROWS = 1024
FFN_COLS = 512
NORM_CHUNK_ROWS = 256

BF16 = jnp.bfloat16
F32 = jnp.float32


def _rms_scale(x):
    return lax.rsqrt(jnp.mean(x * x, axis=-1, keepdims=True) + EPS)


def _gelu_exact(x):
    return 0.5 * x * (1.0 + lax.erf(x * (2.0 ** -0.5)))


def _dot(a, b):
    return jnp.dot(a, b, preferred_element_type=F32)


def _params(*semantics):
    return pltpu.CompilerParams(dimension_semantics=semantics, vmem_limit_bytes=VMEM_LIMIT_BYTES)


def _mem_kv_kernel(mem_ref, g_mem_ref, w_ref, g_mk_ref, kt_ref, v_ref):
    m = mem_ref[...]
    h = (m * _rms_scale(m) * g_mem_ref[...]).astype(BF16)
    kv = _dot(h, w_ref[...])
    for hd in range(MEM_HEADS):
        k = kv[:, hd * HEAD_DIM:(hd + 1) * HEAD_DIM]
        k = k * _rms_scale(k) * g_mk_ref[...]
        kt_ref[hd * HEAD_DIM:(hd + 1) * HEAD_DIM, :] = k.T.astype(BF16)
    v_ref[...] = kv[:, MEM_WIDTH:].astype(BF16)


def _mem_kv(mem, g_mem, w_mem_kv, g_mk):
    batch = mem.shape[0]
    return pl.pallas_call(
        _mem_kv_kernel,
        grid=(DEPTH, batch),
        in_specs=[
            pl.BlockSpec((None, N_MEM, D_MODEL), lambda l, b: (b, 0, 0)),
            pl.BlockSpec((None, 1, D_MODEL), lambda l, b: (l, 0, 0)),
            pl.BlockSpec((None, D_MODEL, 2 * MEM_WIDTH), lambda l, b: (l, 0, 0)),
            pl.BlockSpec((None, 1, HEAD_DIM), lambda l, b: (l, 0, 0)),
        ],
        out_specs=[
            pl.BlockSpec((None, None, MEM_WIDTH, N_MEM), lambda l, b: (l, b, 0, 0)),
            pl.BlockSpec((None, None, N_MEM, MEM_WIDTH), lambda l, b: (l, b, 0, 0)),
        ],
        out_shape=[
            jax.ShapeDtypeStruct((DEPTH, batch, MEM_WIDTH, N_MEM), BF16),
            jax.ShapeDtypeStruct((DEPTH, batch, N_MEM, MEM_WIDTH), BF16),
        ],
        compiler_params=_params("arbitrary", "arbitrary"),
        name="mem_kv",
    )(mem, g_mem.reshape(DEPTH, 1, D_MODEL), w_mem_kv, g_mk.reshape(DEPTH, 1, HEAD_DIM))


def _norm_row_chunks(x_ref, g_ref, h_ref):
    for r in range(0, x_ref.shape[0], NORM_CHUNK_ROWS):
        rows = slice(r, r + NORM_CHUNK_ROWS)
        x = x_ref[rows, :]
        h = (x * _rms_scale(x) * g_ref[...]).astype(BF16)
        h_ref[rows, :] = h
        yield rows, h


def _project_column_chunks(x_ref, g_ref, h_ref, n_cols, project):
    for rows, h in _norm_row_chunks(x_ref, g_ref, h_ref):
        project(h, rows, 0)
    h = h_ref[...]
    for c0 in range(IN_PROJ_DOT_COLS, n_cols, IN_PROJ_DOT_COLS):
        project(h, slice(0, x_ref.shape[0]), c0)


def _in_proj_gmlp_kernel(x_ref, g_ref, w_ref, z_ref, h_ref, *, gelu_cols):
    def project(h, rows, c0):
        cols = slice(c0, c0 + IN_PROJ_DOT_COLS)
        acc = _dot(h, w_ref[:, cols])
        z_ref[rows, cols] = (_gelu_exact(acc) if c0 < gelu_cols else acc).astype(BF16)

    _project_column_chunks(x_ref, g_ref, h_ref, z_ref.shape[1], project)


def _in_proj_attn_kernel(x_ref, g_ref, w_ref, gain_ref, cos_ref, sin_ref, seg_ref, z_ref, h_ref, *, rope_cols):
    def project(h, rows, c0):
        acc = _dot(h, w_ref[:, c0:c0 + IN_PROJ_DOT_COLS])
        if c0 >= rope_cols:
            z_ref[rows, c0:c0 + IN_PROJ_DOT_COLS] = acc.astype(BF16)
            return
        cos, sin = cos_ref[rows, :], sin_ref[rows, :]
        for p0 in range(0, IN_PROJ_DOT_COLS, 2 * HEAD_DIM):
            t0, t1 = acc[:, p0:p0 + HEAD_DIM], acc[:, p0 + HEAD_DIM:p0 + 2 * HEAD_DIM]
            lanes0 = slice(c0 + p0, c0 + p0 + HEAD_DIM)
            lanes1 = slice(c0 + p0 + HEAD_DIM, c0 + p0 + 2 * HEAD_DIM)
            ss = t0 * t0 + t1 * t1
            hi = ss.astype(BF16)
            lo = (ss - hi.astype(F32)).astype(BF16)
            tot = _dot(jnp.concatenate([hi, lo], axis=1), seg_ref[...])
            r = lax.rsqrt(tot * (1.0 / HEAD_DIM) + EPS)
            n0 = t0 * r * gain_ref[:, lanes0]
            n1 = t1 * r * gain_ref[:, lanes1]
            z_ref[rows, lanes0] = (n0 * cos - n1 * sin).astype(BF16)
            z_ref[rows, lanes1] = (n1 * cos + n0 * sin).astype(BF16)

    _project_column_chunks(x_ref, g_ref, h_ref, z_ref.shape[1], project)


def _in_proj(x2d, g, w, layer, *, attn=None):
    rows, _ = x2d.shape
    cols = w.shape[2]
    bm = IN_PROJ_ROWS
    assert cols % IN_PROJ_DOT_COLS == 0
    resident = pl.Buffered(1)
    in_specs = [
        pl.BlockSpec((bm, D_MODEL), lambda i: (i, 0)),
        pl.BlockSpec((1, D_MODEL), lambda i: (0, 0)),
        pl.BlockSpec((None, D_MODEL, cols), lambda i: (layer, 0, 0), pipeline_mode=resident),
    ]
    args = [x2d, g.reshape(1, D_MODEL), w]
    if attn is None:
        assert (2 * TOK_WIDTH) % IN_PROJ_DOT_COLS == 0
        body = functools.partial(_in_proj_gmlp_kernel, gelu_cols=2 * TOK_WIDTH)
    else:
        gain, cos, sin, seg = attn
        seq_tiles = cos.shape[0] // bm
        assert (TOK_WIDTH + KV_WIDTH) % IN_PROJ_DOT_COLS == 0
        body = functools.partial(_in_proj_attn_kernel, rope_cols=TOK_WIDTH + KV_WIDTH)
        in_specs += [
            pl.BlockSpec((1, cols), lambda i: (0, 0)),
            pl.BlockSpec((bm, HEAD_DIM), lambda i: (i % seq_tiles, 0)),
            pl.BlockSpec((bm, HEAD_DIM), lambda i: (i % seq_tiles, 0)),
            pl.BlockSpec((2 * HEAD_DIM, HEAD_DIM), lambda i: (0, 0)),
        ]
        args += [gain, cos, sin, seg]
    return pl.pallas_call(
        body,
        grid=(rows // bm,),
        in_specs=in_specs,
        out_specs=pl.BlockSpec((bm, cols), lambda i: (i, 0)),
        out_shape=jax.ShapeDtypeStruct((rows, cols), BF16),
        scratch_shapes=[pltpu.VMEM((bm, D_MODEL), BF16)],
        compiler_params=_params("parallel"),
        name="in_proj_gmlp" if attn is None else "in_proj_attn",
    )(*args)


_NT_DIMS = (((1,), (1,)), ((), ()))


def _attention_kernel(q_ref, k_ref, v_ref, o_ref, v1_ref):
    @pl.when(pl.program_id(2) == 0)
    def _():
        ones_col = lax.broadcasted_iota(jnp.int32, (v_ref.shape[0], HEAD_DIM), 1) == 0
        v1_ref[:, :HEAD_DIM] = v_ref[...]
        v1_ref[:, HEAD_DIM:] = jnp.where(ones_col, 1.0, 0.0).astype(BF16)

    k = k_ref[...]
    v = v1_ref[...]
    half = HEAD_DIM // 2
    lane = lax.broadcasted_iota(jnp.int32, (ATTN_CHAIN_ROWS, 2 * HEAD_DIM), 1) % HEAD_DIM
    first_lane = (pl.program_id(1) % 2) * half
    keep = jnp.logical_and(lane >= first_lane, lane < first_lane + half)
    for r in range(0, q_ref.shape[0], ATTN_CHAIN_ROWS):
        rows = slice(r, r + ATTN_CHAIN_ROWS)
        for g in range(Q_PER_KV):
            q = q_ref[rows, g * 2 * HEAD_DIM:(g + 1) * 2 * HEAD_DIM]
            q = jnp.where(keep, q, jnp.zeros_like(q))
            s = lax.dot_general(q, k, _NT_DIMS, preferred_element_type=F32)
            p = jnp.exp2((s - jnp.max(s, axis=-1, keepdims=True)).astype(BF16))
            o = _dot(p, v)
            o = o[:, :HEAD_DIM] / o[:, HEAD_DIM:HEAD_DIM + 1]
            o_ref[rows, g * HEAD_DIM:(g + 1) * HEAD_DIM] = o.astype(BF16)


def _attention(z, batch, seq):
    bq = ATTN_Q_ROWS
    q_tiles = seq // bq
    pair_width = 2 * HEAD_DIM
    k_block0 = TOK_WIDTH // pair_width
    v_block0 = (TOK_WIDTH + KV_WIDTH) // HEAD_DIM
    return pl.pallas_call(
        _attention_kernel,
        grid=(batch, KV_HEADS, q_tiles),
        in_specs=[
            pl.BlockSpec((bq, Q_PER_KV * pair_width), lambda b, h, i: (b * q_tiles + i, h // 2)),
            pl.BlockSpec((seq, pair_width), lambda b, h, i: (b, k_block0 + h // 2)),
            pl.BlockSpec((seq, HEAD_DIM), lambda b, h, i: (b, v_block0 + h)),
        ],
        out_specs=pl.BlockSpec((bq, Q_PER_KV * HEAD_DIM), lambda b, h, i: (b * q_tiles + i, h)),
        out_shape=jax.ShapeDtypeStruct((batch * seq, TOK_WIDTH), BF16),
        scratch_shapes=[pltpu.VMEM((seq, 2 * HEAD_DIM), BF16)],
        compiler_params=_params("parallel", "parallel", "arbitrary"),
        name="attention",
    )(z, z, z)


def _memory_attention_into(mixed_ref, qm_ref, g_mq_ref, kt_ref, v_ref):
    for hd in range(MEM_HEADS):
        lanes = slice(hd * HEAD_DIM, (hd + 1) * HEAD_DIM)
        q = qm_ref[:, lanes].astype(F32)
        q = (q * _rms_scale(q) * g_mq_ref[...] * SCORE_SCALE).astype(BF16)
        s = _dot(q, kt_ref[lanes, :])
        p = jnp.exp(s - jnp.max(s, axis=-1, keepdims=True))
        denom = jnp.sum(p, axis=-1, keepdims=True)
        o = _dot(p.astype(BF16), v_ref[:, lanes]) / denom
        mixed_ref[:, TOK_WIDTH + hd * HEAD_DIM:TOK_WIDTH + (hd + 1) * HEAD_DIM] = o.astype(BF16)


def _mix_out_gmlp_kernel(u_ref, v_ref, qm_ref, g_v_ref, ws_ref, bs_ref, g_mq_ref, kt_ref, mv_ref,
                         w_out_ref, x_ref, o_ref, mixed_ref):
    v = v_ref[...].astype(F32)
    vn = (v * _rms_scale(v) * g_v_ref[...]).astype(BF16)
    for c in range(v.shape[0] // CHUNK):
        rows = slice(c * CHUNK, (c + 1) * CHUNK)
        for g in range(A_GROUPS):
            lanes = slice(g * HEAD_DIM, (g + 1) * HEAD_DIM)
            s = _dot(ws_ref[g], vn[rows, lanes]) + bs_ref[g]
            mixed_ref[rows, lanes] = (u_ref[rows, lanes].astype(F32) * s).astype(BF16)
    _memory_attention_into(mixed_ref, qm_ref, g_mq_ref, kt_ref, mv_ref)
    o_ref[...] = x_ref[...] + _dot(mixed_ref[...], w_out_ref[...])


def _mix_out_attn_kernel(t_ref, qm_ref, g_mq_ref, kt_ref, mv_ref, w_out_ref, x_ref, o_ref, mixed_ref):
    mixed_ref[:, :TOK_WIDTH] = t_ref[...]
    _memory_attention_into(mixed_ref, qm_ref, g_mq_ref, kt_ref, mv_ref)
    o_ref[...] = x_ref[...] + _dot(mixed_ref[...], w_out_ref[...])


def _mix_out(x2d, z, mem_kt, mem_v, g_mq, w_out, layer, seq, *, gmlp=None, tok=None):
    rows = x2d.shape[0]
    bm = MIX_ROWS_GMLP if gmlp is not None else MIX_ROWS_ATTN
    seq_tiles = seq // bm
    qm_col = (z.shape[1] - MEM_WIDTH) // MEM_WIDTH
    shared_specs = [
        pl.BlockSpec((bm, MEM_WIDTH), lambda i: (i, qm_col)),
    ]
    tail_specs = [
        pl.BlockSpec((1, HEAD_DIM), lambda i: (0, 0)),
        pl.BlockSpec((None, None, MEM_WIDTH, N_MEM), lambda i: (layer, i // seq_tiles, 0, 0)),
        pl.BlockSpec((None, None, N_MEM, MEM_WIDTH), lambda i: (layer, i // seq_tiles, 0, 0)),
        pl.BlockSpec((None, D_MODEL, D_MODEL), lambda i: (layer, 0, 0), pipeline_mode=pl.Buffered(1)),
        pl.BlockSpec((bm, D_MODEL), lambda i: (i, 0)),
    ]
    tail_args = [g_mq.reshape(1, HEAD_DIM), mem_kt, mem_v, w_out, x2d]
    if gmlp is not None:
        g_v, w_s, b_s, gmlp_layer = gmlp
        body = _mix_out_gmlp_kernel
        in_specs = [
            pl.BlockSpec((bm, TOK_WIDTH), lambda i: (i, 0)),
            pl.BlockSpec((bm, TOK_WIDTH), lambda i: (i, 1)),
        ] + shared_specs + [
            pl.BlockSpec((1, TOK_WIDTH), lambda i: (0, 0)),
            pl.BlockSpec((None, A_GROUPS, CHUNK, CHUNK), lambda i: (gmlp_layer, 0, 0, 0)),
            pl.BlockSpec((A_GROUPS, CHUNK, 1), lambda i: (0, 0, 0)),
        ] + tail_specs
        args = [z, z, z, g_v.reshape(1, TOK_WIDTH), w_s, b_s.reshape(A_GROUPS, CHUNK, 1)] + tail_args
        name = "mix_out_gmlp"
    else:
        body = _mix_out_attn_kernel
        in_specs = [pl.BlockSpec((bm, TOK_WIDTH), lambda i: (i, 0))] + shared_specs + tail_specs
        args = [tok, z] + tail_args
        name = "mix_out_attn"
    return pl.pallas_call(
        body,
        grid=(rows // bm,),
        in_specs=in_specs,
        out_specs=pl.BlockSpec((bm, D_MODEL), lambda i: (i, 0)),
        out_shape=jax.ShapeDtypeStruct((rows, D_MODEL), F32),
        scratch_shapes=[pltpu.VMEM((bm, D_MODEL), BF16)],
        compiler_params=_params("parallel"),
        name=name,
    )(*args)


def _ffn_kernel(x_ref, g_ref, wg_ref, wu_ref, wd_ref, o_ref, h_ref):
    j = pl.program_id(1)

    def contribution(h):
        gate = _dot(h, wg_ref[...])
        up = _dot(h, wu_ref[...])
        act = (jax.nn.silu(gate) * up).astype(BF16)
        return _dot(act, wd_ref[...])

    @pl.when(j == 0)
    def _():
        for rows, h in _norm_row_chunks(x_ref, g_ref, h_ref):
            o_ref[rows, :] = x_ref[rows, :] + contribution(h)

    @pl.when(j > 0)
    def _():
        o_ref[...] += contribution(h_ref[...])


def _ffn(x2d, g, w_gate_up, w_down, layer):
    rows = x2d.shape[0]
    d_ff = w_down.shape[1]
    bm, bf = FFN_ROWS, FFN_COLS
    ff_tiles = d_ff // bf
    return pl.pallas_call(
        _ffn_kernel,
        grid=(rows // bm, ff_tiles),
        in_specs=[
            pl.BlockSpec((bm, D_MODEL), lambda i, j: (i, 0)),
            pl.BlockSpec((1, D_MODEL), lambda i, j: (0, 0)),
            pl.BlockSpec((None, D_MODEL, bf), lambda i, j: (layer, 0, j)),
            pl.BlockSpec((None, D_MODEL, bf), lambda i, j: (layer, 0, ff_tiles + j)),
            pl.BlockSpec((None, bf, D_MODEL), lambda i, j: (layer, j, 0)),
        ],
        out_specs=pl.BlockSpec((bm, D_MODEL), lambda i, j: (i, 0)),
        out_shape=jax.ShapeDtypeStruct((rows, D_MODEL), F32),
        scratch_shapes=[pltpu.VMEM((bm, D_MODEL), BF16)],
        compiler_params=_params("parallel", "arbitrary"),
        name="ffn",
    )(x2d, g.reshape(1, D_MODEL), w_gate_up, w_gate_up, w_down)


def _pair_layout_columns(a, kv_groups):
    lead = a.shape[:-1]
    n = len(lead)
    a = a.reshape(lead + (KV_HEADS // 2, 2, kv_groups, 2, 2, ROPE_PAIRS))
    a = a.transpose(tuple(range(n)) + (n, n + 2, n + 4, n + 1, n + 3, n + 5))
    return a.reshape(lead + (KV_HEADS * kv_groups * HEAD_DIM,))


def _pair_layout_weights(w):
    q, k, rest = w[..., :TOK_WIDTH], w[..., TOK_WIDTH:TOK_WIDTH + KV_WIDTH], w[..., TOK_WIDTH + KV_WIDTH:]
    return jnp.concatenate([_pair_layout_columns(q, Q_PER_KV), _pair_layout_columns(k, 1), rest], axis=-1)


def _attn_tables(seq):
    n_rows = seq // GRID_W
    rows = jnp.broadcast_to(jnp.arange(n_rows)[:, None], (n_rows, GRID_W)).reshape(seq)
    cols = jnp.broadcast_to(jnp.arange(GRID_W)[None, :], (n_rows, GRID_W)).reshape(seq)
    freqs = ROPE_THETA ** (-jnp.arange(ROPE_PAIRS, dtype=F32) / ROPE_PAIRS)
    ang_r = rows.astype(F32)[:, None] * freqs
    ang_c = cols.astype(F32)[:, None] * freqs
    ang = jnp.concatenate([ang_r, ang_c, ang_r, ang_c], axis=-1)
    k_seg = (jnp.arange(2 * HEAD_DIM) % HEAD_DIM) // (HEAD_DIM // 2)
    n_seg = jnp.arange(HEAD_DIM) // (HEAD_DIM // 2)
    seg = (k_seg[:, None] == n_seg[None, :]).astype(BF16)
    return jnp.cos(ang), jnp.sin(ang), seg


def _attn_gains(g_q, g_k, cols):
    gq = _pair_layout_columns(jnp.tile(g_q * (SCORE_SCALE * LOG2_E), Q_HEADS), Q_PER_KV)
    gk = _pair_layout_columns(jnp.tile(g_k, KV_HEADS), 1)
    return jnp.concatenate([gq, gk, jnp.ones((cols - TOK_WIDTH - KV_WIDTH,), F32)]).reshape(1, cols)


def kernel(x, mem, g_mix, g_ffn, w_in_a, g_v_a, w_spatial, b_spatial, w_in_b, g_q_b, g_k_b, g_mem, w_mem_kv,
           g_mq, g_mk, w_out, w_gate_up, w_down):
    batch, seq, _ = x.shape
    rows = batch * seq
    assert seq % IN_PROJ_ROWS == 0 and seq % ATTN_Q_ROWS == 0 and rows % FFN_ROWS == 0
    assert seq % MIX_ROWS_GMLP == 0 and seq % MIX_ROWS_ATTN == 0 and MIX_ROWS_GMLP % CHUNK == 0
    assert w_down.shape[1] % FFN_COLS == 0

    cos, sin, seg = _attn_tables(seq)
    mem_kt, mem_v = _mem_kv(mem, g_mem, w_mem_kv.astype(BF16), g_mk)
    w_in_a, w_spatial, w_out = w_in_a.astype(BF16), w_spatial.astype(BF16), w_out.astype(BF16)
    w_in_b = _pair_layout_weights(w_in_b.astype(BF16))
    w_gate_up, w_down = w_gate_up.astype(BF16), w_down.astype(BF16)
    xs = x.reshape(rows, D_MODEL)
    for l in range(DEPTH):
        idx = l // 2
        if l % 2 == 0:
            z = _in_proj(xs, g_mix[l], w_in_a, idx)
            xs = _mix_out(xs, z, mem_kt, mem_v, g_mq[l], w_out, l, seq,
                          gmlp=(g_v_a[idx], w_spatial, b_spatial[idx], idx))
        else:
            gain = _attn_gains(g_q_b[idx], g_k_b[idx], w_in_b.shape[2])
            z = _in_proj(xs, g_mix[l], w_in_b, idx, attn=(gain, cos, sin, seg))
            tok = _attention(z, batch, seq)
            xs = _mix_out(xs, z, mem_kt, mem_v, g_mq[l], w_out, l, seq, tok=tok)
        xs = _ffn(xs, g_ffn[l], w_gate_up, w_down, l)
    return xs.reshape(batch, seq, D_MODEL)
```

```python
import functools

import jax
import jax.numpy as jnp
from jax import lax
from jax.experimental import pallas as pl
from jax.experimental.pallas import tpu as pltpu

D_MODEL = 2048
DEPTH = 4
N_MEM = 256
GRID_W = 64
HEAD_DIM = 128
MEM_HEADS = 4
MEM_WIDTH = MEM_HEADS * HEAD_DIM
TOK_WIDTH = D_MODEL - MEM_WIDTH
CHUNK = 128
A_GROUPS = TOK_WIDTH // HEAD_DIM
Q_HEADS = TOK_WIDTH // HEAD_DIM
KV_HEADS = 4
Q_PER_KV = Q_HEADS // KV_HEADS
KV_WIDTH = KV_HEADS * HEAD_DIM
ROPE_THETA = 10000.0
ROPE_PAIRS = HEAD_DIM // 4
EPS = 1e-6
SCORE_SCALE = HEAD_DIM ** -0.5

VMEM_LIMIT_BYTES = 63 * 1024 * 1024
MXU_COLS = 256
LOG2_E = 1.4426950408889634

IN_PROJ_ROWS = 1024
IN_PROJ_DOT_COLS = 512
MIX_ROWS_GMLP = 512
MIX_ROWS_ATTN = 1024
ATTN_Q_ROWS = 1024
ATTN_CHAIN_ROWS = 256
FFN_ROWS = 1024
FFN_COLS = 512
NORM_CHUNK_ROWS = 256

BF16 = jnp.bfloat16
F32 = jnp.float32


def _rms_scale(x):
    return lax.rsqrt(jnp.mean(x * x, axis=-1, keepdims=True) + EPS)


def _gelu_exact(x):
    return 0.5 * x * (1.0 + lax.erf(x * (2.0 ** -0.5)))


def _dot(a, b):
    return jnp.dot(a, b, preferred_element_type=F32)


def _params(*semantics):
    return pltpu.CompilerParams(dimension_semantics=semantics, vmem_limit_bytes=VMEM_LIMIT_BYTES)


BF16_TILE_ROWS = 16
LANES = 128


def _weight_cast_specs(w, layer, n_steps, step_id):
    _, rows, cols = w.shape
    n_col = 1 if rows % (BF16_TILE_ROWS * n_steps) == 0 else 2
    n_row = n_steps // n_col
    assert n_row * n_col == n_steps and rows % (BF16_TILE_ROWS * n_row) == 0 and cols % (LANES * n_col) == 0
    block = (rows // n_row, cols // n_col)
    src = pl.BlockSpec((None,) + block, lambda *g: (layer, step_id(*g) // n_col, step_id(*g) % n_col))
    dst = pl.BlockSpec(block, lambda *g: (step_id(*g) // n_col, step_id(*g) % n_col))
    return src, dst, jax.ShapeDtypeStruct((rows, cols), BF16)


def _with_weight_casts(body, n_in, n_out, n_cast):
    def kernel(*refs):
        ins, refs = refs[:n_in], refs[n_in:]
        srcs, refs = refs[:n_cast], refs[n_cast:]
        outs, refs = refs[:n_out], refs[n_out:]
        dsts, scratch = refs[:n_cast], refs[n_cast:]
        for src, dst in zip(srcs, dsts):
            dst[...] = src[...].astype(BF16)
        body(*ins, *outs, *scratch)
    return kernel


def _mem_kv_kernel(mem_ref, g_mem_ref, w_ref, g_mk_ref, kt_ref, v_ref):
    m = mem_ref[...]
    h = (m * _rms_scale(m) * g_mem_ref[...]).astype(BF16)
    kv = _dot(h, w_ref[...])
    for hd in range(MEM_HEADS):
        k = kv[:, hd * HEAD_DIM:(hd + 1) * HEAD_DIM]
        k = k * _rms_scale(k) * g_mk_ref[...]
        kt_ref[hd * HEAD_DIM:(hd + 1) * HEAD_DIM, :] = k.T.astype(BF16)
    v_ref[...] = kv[:, MEM_WIDTH:].astype(BF16)


def _mem_kv(mem, g_mem, w_mem_kv, g_mk):
    batch = mem.shape[0]
    return pl.pallas_call(
        _mem_kv_kernel,
        grid=(DEPTH, batch),
        in_specs=[
            pl.BlockSpec((None, N_MEM, D_MODEL), lambda l, b: (b, 0, 0)),
            pl.BlockSpec((None, 1, D_MODEL), lambda l, b: (l, 0, 0)),
            pl.BlockSpec((None, D_MODEL, 2 * MEM_WIDTH), lambda l, b: (l, 0, 0)),
            pl.BlockSpec((None, 1, HEAD_DIM), lambda l, b: (l, 0, 0)),
        ],
        out_specs=[
            pl.BlockSpec((None, None, MEM_WIDTH, N_MEM), lambda l, b: (l, b, 0, 0)),
            pl.BlockSpec((None, None, N_MEM, MEM_WIDTH), lambda l, b: (l, b, 0, 0)),
        ],
        out_shape=[
            jax.ShapeDtypeStruct((DEPTH, batch, MEM_WIDTH, N_MEM), BF16),
            jax.ShapeDtypeStruct((DEPTH, batch, N_MEM, MEM_WIDTH), BF16),
        ],
        compiler_params=_params("arbitrary", "arbitrary"),
        name="mem_kv",
    )(mem, g_mem.reshape(DEPTH, 1, D_MODEL), w_mem_kv, g_mk.reshape(DEPTH, 1, HEAD_DIM))


def _norm_row_chunks(x_ref, g_ref, h_ref):
    for r in range(0, x_ref.shape[0], NORM_CHUNK_ROWS):
        rows = slice(r, r + NORM_CHUNK_ROWS)
        x = x_ref[rows, :]
        h = (x * _rms_scale(x) * g_ref[...]).astype(BF16)
        h_ref[rows, :] = h
        yield rows, h


def _project_column_chunks(x_ref, g_ref, h_ref, n_cols, project):
    for rows, h in _norm_row_chunks(x_ref, g_ref, h_ref):
        project(h, rows, 0)
    h = h_ref[...]
    for c0 in range(IN_PROJ_DOT_COLS, n_cols, IN_PROJ_DOT_COLS):
        project(h, slice(0, x_ref.shape[0]), c0)


def _in_proj_gmlp_kernel(x_ref, g_ref, w_ref, z_ref, h_ref, *, gelu_cols):
    def project(h, rows, c0):
        cols = slice(c0, c0 + IN_PROJ_DOT_COLS)
        acc = _dot(h, w_ref[:, cols])
        z_ref[rows, cols] = (_gelu_exact(acc) if c0 < gelu_cols else acc).astype(BF16)

    _project_column_chunks(x_ref, g_ref, h_ref, z_ref.shape[1], project)


def _in_proj_attn_kernel(x_ref, g_ref, w_ref, gain_ref, cos_ref, sin_ref, seg_ref, z_ref, h_ref, *, rope_cols):
    def project(h, rows, c0):
        acc = _dot(h, w_ref[:, c0:c0 + IN_PROJ_DOT_COLS])
        if c0 >= rope_cols:
            z_ref[rows, c0:c0 + IN_PROJ_DOT_COLS] = acc.astype(BF16)
            return
        cos, sin = cos_ref[rows, :], sin_ref[rows, :]
        for p0 in range(0, IN_PROJ_DOT_COLS, 2 * HEAD_DIM):
            t0, t1 = acc[:, p0:p0 + HEAD_DIM], acc[:, p0 + HEAD_DIM:p0 + 2 * HEAD_DIM]
            lanes0 = slice(c0 + p0, c0 + p0 + HEAD_DIM)
            lanes1 = slice(c0 + p0 + HEAD_DIM, c0 + p0 + 2 * HEAD_DIM)
            ss = t0 * t0 + t1 * t1
            hi = ss.astype(BF16)
            lo = (ss - hi.astype(F32)).astype(BF16)
            tot = _dot(jnp.concatenate([hi, lo], axis=1), seg_ref[...])
            r = lax.rsqrt(tot * (1.0 / HEAD_DIM) + EPS)
            n0 = t0 * r * gain_ref[:, lanes0]
            n1 = t1 * r * gain_ref[:, lanes1]
            z_ref[rows, lanes0] = (n0 * cos - n1 * sin).astype(BF16)
            z_ref[rows, lanes1] = (n1 * cos + n0 * sin).astype(BF16)

    _project_column_chunks(x_ref, g_ref, h_ref, z_ref.shape[1], project)


def _in_proj(x2d, g, w, layer, *, attn=None):
    rows, _ = x2d.shape
    cols = w.shape[2]
    bm = IN_PROJ_ROWS
    assert cols % IN_PROJ_DOT_COLS == 0
    resident = pl.Buffered(1)
    in_specs = [
        pl.BlockSpec((bm, D_MODEL), lambda i: (i, 0)),
        pl.BlockSpec((1, D_MODEL), lambda i: (0, 0)),
        pl.BlockSpec((None, D_MODEL, cols), lambda i: (layer, 0, 0), pipeline_mode=resident),
    ]
    args = [x2d, g.reshape(1, D_MODEL), w]
    if attn is None:
        assert (2 * TOK_WIDTH) % IN_PROJ_DOT_COLS == 0
        body = functools.partial(_in_proj_gmlp_kernel, gelu_cols=2 * TOK_WIDTH)
    else:
        gain, cos, sin, seg = attn
        seq_tiles = cos.shape[0] // bm
        assert (TOK_WIDTH + KV_WIDTH) % IN_PROJ_DOT_COLS == 0
        body = functools.partial(_in_proj_attn_kernel, rope_cols=TOK_WIDTH + KV_WIDTH)
        in_specs += [
            pl.BlockSpec((1, cols), lambda i: (0, 0)),
            pl.BlockSpec((bm, HEAD_DIM), lambda i: (i % seq_tiles, 0)),
            pl.BlockSpec((bm, HEAD_DIM), lambda i: (i % seq_tiles, 0)),
            pl.BlockSpec((2 * HEAD_DIM, HEAD_DIM), lambda i: (0, 0)),
        ]
        args += [gain, cos, sin, seg]
    return pl.pallas_call(
        body,
        grid=(rows // bm,),
        in_specs=in_specs,
        out_specs=pl.BlockSpec((bm, cols), lambda i: (i, 0)),
        out_shape=jax.ShapeDtypeStruct((rows, cols), BF16),
        scratch_shapes=[pltpu.VMEM((bm, D_MODEL), BF16)],
        compiler_params=_params("parallel"),
        name="in_proj_gmlp" if attn is None else "in_proj_attn",
    )(*args)


_NT_DIMS = (((1,), (1,)), ((), ()))


def _attention_kernel(q_ref, k_ref, v_ref, o_ref, v1_ref):
    @pl.when(pl.program_id(2) == 0)
    def _():
        ones_col = lax.broadcasted_iota(jnp.int32, (v_ref.shape[0], HEAD_DIM), 1) == 0
        v1_ref[:, :HEAD_DIM] = v_ref[...]
        v1_ref[:, HEAD_DIM:] = jnp.where(ones_col, 1.0, 0.0).astype(BF16)

    k = k_ref[...]
    v = v1_ref[...]
    half = HEAD_DIM // 2
    lane = lax.broadcasted_iota(jnp.int32, (ATTN_CHAIN_ROWS, 2 * HEAD_DIM), 1) % HEAD_DIM
    first_lane = (pl.program_id(1) % 2) * half
    keep = jnp.logical_and(lane >= first_lane, lane < first_lane + half)
    for r in range(0, q_ref.shape[0], ATTN_CHAIN_ROWS):
        rows = slice(r, r + ATTN_CHAIN_ROWS)
        for g in range(Q_PER_KV):
            q = q_ref[rows, g * 2 * HEAD_DIM:(g + 1) * 2 * HEAD_DIM]
            q = jnp.where(keep, q, jnp.zeros_like(q))
            s = lax.dot_general(q, k, _NT_DIMS, preferred_element_type=F32)
            p = jnp.exp2((s - jnp.max(s, axis=-1, keepdims=True)).astype(BF16))
            o = _dot(p, v)
            o = o[:, :HEAD_DIM] / o[:, HEAD_DIM:HEAD_DIM + 1]
            o_ref[rows, g * HEAD_DIM:(g + 1) * HEAD_DIM] = o.astype(BF16)


def _attention(z, batch, seq, casts=()):
    bq = ATTN_Q_ROWS
    q_tiles = seq // bq
    pair_width = 2 * HEAD_DIM
    k_block0 = TOK_WIDTH // pair_width
    v_block0 = (TOK_WIDTH + KV_WIDTH) // HEAD_DIM
    grid = (batch, KV_HEADS, q_tiles)
    cast_specs = [_weight_cast_specs(w, layer, batch * KV_HEADS * q_tiles,
                                     lambda b, h, i: (b * KV_HEADS + h) * q_tiles + i) for w, layer in casts]
    in_specs = [
        pl.BlockSpec((bq, Q_PER_KV * pair_width), lambda b, h, i: (b * q_tiles + i, h // 2)),
        pl.BlockSpec((seq, pair_width), lambda b, h, i: (b, k_block0 + h // 2)),
        pl.BlockSpec((seq, HEAD_DIM), lambda b, h, i: (b, v_block0 + h)),
    ]
    return pl.pallas_call(
        _with_weight_casts(_attention_kernel, len(in_specs), 1, len(casts)),
        grid=grid,
        in_specs=in_specs + [src for src, _, _ in cast_specs],
        out_specs=[pl.BlockSpec((bq, Q_PER_KV * HEAD_DIM), lambda b, h, i: (b * q_tiles + i, h))]
        + [dst for _, dst, _ in cast_specs],
        out_shape=[jax.ShapeDtypeStruct((batch * seq, TOK_WIDTH), BF16)] + [shape for _, _, shape in cast_specs],
        scratch_shapes=[pltpu.VMEM((seq, 2 * HEAD_DIM), BF16)],
        compiler_params=_params("parallel", "parallel", "arbitrary"),
        name="attention",
    )(z, z, z, *[w for w, _ in casts])


def _memory_attention_into(mixed_ref, qm_ref, g_mq_ref, kt_ref, v_ref):
    for hd in range(MEM_HEADS):
        lanes = slice(hd * HEAD_DIM, (hd + 1) * HEAD_DIM)
        q = qm_ref[:, lanes].astype(F32)
        q = (q * _rms_scale(q) * g_mq_ref[...] * SCORE_SCALE).astype(BF16)
        s = _dot(q, kt_ref[lanes, :])
        p = jnp.exp(s - jnp.max(s, axis=-1, keepdims=True))
        denom = jnp.sum(p, axis=-1, keepdims=True)
        o = _dot(p.astype(BF16), v_ref[:, lanes]) / denom
        mixed_ref[:, TOK_WIDTH + hd * HEAD_DIM:TOK_WIDTH + (hd + 1) * HEAD_DIM] = o.astype(BF16)


def _mix_out_gmlp_kernel(u_ref, v_ref, qm_ref, g_v_ref, ws_ref, bs_ref, g_mq_ref, kt_ref, mv_ref,
                         w_out_ref, x_ref, o_ref, mixed_ref):
    v = v_ref[...].astype(F32)
    vn = (v * _rms_scale(v) * g_v_ref[...]).astype(BF16)
    for c in range(v.shape[0] // CHUNK):
        rows = slice(c * CHUNK, (c + 1) * CHUNK)
        for g in range(A_GROUPS):
            lanes = slice(g * HEAD_DIM, (g + 1) * HEAD_DIM)
            s = _dot(ws_ref[g], vn[rows, lanes]) + bs_ref[g]
            mixed_ref[rows, lanes] = (u_ref[rows, lanes].astype(F32) * s).astype(BF16)
    _memory_attention_into(mixed_ref, qm_ref, g_mq_ref, kt_ref, mv_ref)
    o_ref[...] = x_ref[...] + _dot(mixed_ref[...], w_out_ref[...])


def _mix_out_attn_kernel(t_ref, qm_ref, g_mq_ref, kt_ref, mv_ref, w_out_ref, x_ref, o_ref, mixed_ref):
    mixed_ref[:, :TOK_WIDTH] = t_ref[...]
    _memory_attention_into(mixed_ref, qm_ref, g_mq_ref, kt_ref, mv_ref)
    o_ref[...] = x_ref[...] + _dot(mixed_ref[...], w_out_ref[...])


def _mix_out(x2d, z, mem_kt, mem_v, g_mq, w_out, layer, seq, *, gmlp=None, tok=None, casts=()):
    rows = x2d.shape[0]
    bm = MIX_ROWS_GMLP if gmlp is not None else MIX_ROWS_ATTN
    seq_tiles = seq // bm
    qm_col = (z.shape[1] - MEM_WIDTH) // MEM_WIDTH
    shared_specs = [
        pl.BlockSpec((bm, MEM_WIDTH), lambda i: (i, qm_col)),
    ]
    tail_specs = [
        pl.BlockSpec((1, HEAD_DIM), lambda i: (0, 0)),
        pl.BlockSpec((None, None, MEM_WIDTH, N_MEM), lambda i: (layer, i // seq_tiles, 0, 0)),
        pl.BlockSpec((None, None, N_MEM, MEM_WIDTH), lambda i: (layer, i // seq_tiles, 0, 0)),
        pl.BlockSpec((None, D_MODEL, D_MODEL), lambda i: (layer, 0, 0), pipeline_mode=pl.Buffered(1)),
        pl.BlockSpec((bm, D_MODEL), lambda i: (i, 0)),
    ]
    tail_args = [g_mq.reshape(1, HEAD_DIM), mem_kt, mem_v, w_out, x2d]
    if gmlp is not None:
        g_v, w_s, b_s, gmlp_layer = gmlp
        body = _mix_out_gmlp_kernel
        in_specs = [
            pl.BlockSpec((bm, TOK_WIDTH), lambda i: (i, 0)),
            pl.BlockSpec((bm, TOK_WIDTH), lambda i: (i, 1)),
        ] + shared_specs + [
            pl.BlockSpec((1, TOK_WIDTH), lambda i: (0, 0)),
            pl.BlockSpec((None, A_GROUPS, CHUNK, CHUNK), lambda i: (gmlp_layer, 0, 0, 0)),
            pl.BlockSpec((A_GROUPS, CHUNK, 1), lambda i: (0, 0, 0)),
        ] + tail_specs
        args = [z, z, z, g_v.reshape(1, TOK_WIDTH), w_s, b_s.reshape(A_GROUPS, CHUNK, 1)] + tail_args
        name = "mix_out_gmlp"
    else:
        body = _mix_out_attn_kernel
        in_specs = [pl.BlockSpec((bm, TOK_WIDTH), lambda i: (i, 0))] + shared_specs + tail_specs
        args = [tok, z] + tail_args
        name = "mix_out_attn"
    cast_specs = [_weight_cast_specs(w, cast_layer, rows // bm, lambda i: i) for w, cast_layer in casts]
    return pl.pallas_call(
        _with_weight_casts(body, len(in_specs), 1, len(casts)),
        grid=(rows // bm,),
        in_specs=in_specs + [src for src, _, _ in cast_specs],
        out_specs=[pl.BlockSpec((bm, D_MODEL), lambda i: (i, 0))] + [dst for _, dst, _ in cast_specs],
        out_shape=[jax.ShapeDtypeStruct((rows, D_MODEL), F32)] + [shape for _, _, shape in cast_specs],
        scratch_shapes=[pltpu.VMEM((bm, D_MODEL), BF16)],
        compiler_params=_params("parallel"),
        name=name,
    )(*args, *[w for w, _ in casts])


def _ffn_kernel(x_ref, g_ref, wg_ref, wu_ref, wd_ref, o_ref, h_ref):
    j = pl.program_id(1)

    def contribution(h):
        gate = _dot(h, wg_ref[...])
        up = _dot(h, wu_ref[...])
        act = (jax.nn.silu(gate) * up).astype(BF16)
        return _dot(act, wd_ref[...])

    @pl.when(j == 0)
    def _():
        for rows, h in _norm_row_chunks(x_ref, g_ref, h_ref):
            o_ref[rows, :] = x_ref[rows, :] + contribution(h)

    @pl.when(j > 0)
    def _():
        o_ref[...] += contribution(h_ref[...])


def _ffn(x2d, g, w_gate_up, w_down):
    rows = x2d.shape[0]
    d_ff = w_down.shape[0]
    bm, bf = FFN_ROWS, FFN_COLS
    ff_tiles = d_ff // bf
    return pl.pallas_call(
        _ffn_kernel,
        grid=(rows // bm, ff_tiles),
        in_specs=[
            pl.BlockSpec((bm, D_MODEL), lambda i, j: (i, 0)),
            pl.BlockSpec((1, D_MODEL), lambda i, j: (0, 0)),
            pl.BlockSpec((D_MODEL, bf), lambda i, j: (0, j)),
            pl.BlockSpec((D_MODEL, bf), lambda i, j: (0, ff_tiles + j)),
            pl.BlockSpec((bf, D_MODEL), lambda i, j: (j, 0)),
        ],
        out_specs=pl.BlockSpec((bm, D_MODEL), lambda i, j: (i, 0)),
        out_shape=jax.ShapeDtypeStruct((rows, D_MODEL), F32),
        scratch_shapes=[pltpu.VMEM((bm, D_MODEL), BF16)],
        compiler_params=_params("parallel", "arbitrary"),
        name="ffn",
    )(x2d, g.reshape(1, D_MODEL), w_gate_up, w_gate_up, w_down)


def _pair_layout_columns(a, kv_groups):
    lead = a.shape[:-1]
    n = len(lead)
    a = a.reshape(lead + (KV_HEADS // 2, 2, kv_groups, 2, 2, ROPE_PAIRS))
    a = a.transpose(tuple(range(n)) + (n, n + 2, n + 4, n + 1, n + 3, n + 5))
    return a.reshape(lead + (KV_HEADS * kv_groups * HEAD_DIM,))


def _pair_layout_weights(w):
    q, k, rest = w[..., :TOK_WIDTH], w[..., TOK_WIDTH:TOK_WIDTH + KV_WIDTH], w[..., TOK_WIDTH + KV_WIDTH:]
    return jnp.concatenate([_pair_layout_columns(q, Q_PER_KV), _pair_layout_columns(k, 1), rest], axis=-1)


def _attn_tables(seq):
    n_rows = seq // GRID_W
    rows = jnp.broadcast_to(jnp.arange(n_rows)[:, None], (n_rows, GRID_W)).reshape(seq)
    cols = jnp.broadcast_to(jnp.arange(GRID_W)[None, :], (n_rows, GRID_W)).reshape(seq)
    freqs = ROPE_THETA ** (-jnp.arange(ROPE_PAIRS, dtype=F32) / ROPE_PAIRS)
    ang_r = rows.astype(F32)[:, None] * freqs
    ang_c = cols.astype(F32)[:, None] * freqs
    ang = jnp.concatenate([ang_r, ang_c, ang_r, ang_c], axis=-1)
    k_seg = (jnp.arange(2 * HEAD_DIM) % HEAD_DIM) // (HEAD_DIM // 2)
    n_seg = jnp.arange(HEAD_DIM) // (HEAD_DIM // 2)
    seg = (k_seg[:, None] == n_seg[None, :]).astype(BF16)
    return jnp.cos(ang), jnp.sin(ang), seg


def _attn_gains(g_q, g_k, cols):
    gq = _pair_layout_columns(jnp.tile(g_q * (SCORE_SCALE * LOG2_E), Q_HEADS), Q_PER_KV)
    gk = _pair_layout_columns(jnp.tile(g_k, KV_HEADS), 1)
    return jnp.concatenate([gq, gk, jnp.ones((cols - TOK_WIDTH - KV_WIDTH,), F32)]).reshape(1, cols)


def kernel(x, mem, g_mix, g_ffn, w_in_a, g_v_a, w_spatial, b_spatial, w_in_b, g_q_b, g_k_b, g_mem, w_mem_kv,
           g_mq, g_mk, w_out, w_gate_up, w_down):
    batch, seq, _ = x.shape
    rows = batch * seq
    assert seq % IN_PROJ_ROWS == 0 and seq % ATTN_Q_ROWS == 0 and rows % FFN_ROWS == 0
    assert seq % MIX_ROWS_GMLP == 0 and seq % MIX_ROWS_ATTN == 0 and MIX_ROWS_GMLP % CHUNK == 0
    assert w_down.shape[1] % FFN_COLS == 0

    cos, sin, seg = _attn_tables(seq)
    mem_kt, mem_v = _mem_kv(mem, g_mem, w_mem_kv.astype(BF16), g_mk)
    w_in_a, w_spatial, w_out = w_in_a.astype(BF16), w_spatial.astype(BF16), w_out.astype(BF16)
    w_in_b = _pair_layout_weights(w_in_b.astype(BF16))
    xs = x.reshape(rows, D_MODEL)
    for l in range(DEPTH):
        idx = l // 2
        ffn_casts = ((w_gate_up, l), (w_down, l))
        if l % 2 == 0:
            z = _in_proj(xs, g_mix[l], w_in_a, idx)
            xs, ffn_gate_up, ffn_down = _mix_out(xs, z, mem_kt, mem_v, g_mq[l], w_out, l, seq,
                                                 gmlp=(g_v_a[idx], w_spatial, b_spatial[idx], idx),
                                                 casts=ffn_casts)
        else:
            gain = _attn_gains(g_q_b[idx], g_k_b[idx], w_in_b.shape[2])
            z = _in_proj(xs, g_mix[l], w_in_b, idx, attn=(gain, cos, sin, seg))
            tok, ffn_gate_up, ffn_down = _attention(z, batch, seq, casts=ffn_casts)
            xs, = _mix_out(xs, z, mem_kt, mem_v, g_mq[l], w_out, l, seq, tok=tok)
        xs = _ffn(xs, g_ffn[l], ffn_gate_up, ffn_down)
    return xs.reshape(batch, seq, D_MODEL)
```

```python
import functools

import jax
import jax.numpy as jnp
from jax import lax
from jax.experimental import pallas as pl
from jax.experimental.pallas import tpu as pltpu

D_MODEL = 2048
DEPTH = 4
N_MEM = 256
GRID_W = 64
HEAD_DIM = 128
MEM_HEADS = 4
MEM_WIDTH = MEM_HEADS * HEAD_DIM
TOK_WIDTH = D_MODEL - MEM_WIDTH
CHUNK = 128
A_GROUPS = TOK_WIDTH // HEAD_DIM
Q_HEADS = TOK_WIDTH // HEAD_DIM
KV_HEADS = 4
Q_PER_KV = Q_HEADS // KV_HEADS
KV_WIDTH = KV_HEADS * HEAD_DIM
ROPE_THETA = 10000.0
ROPE_PAIRS = HEAD_DIM // 4
EPS = 1e-6
SCORE_SCALE = HEAD_DIM ** -0.5

VMEM_LIMIT_BYTES = 63 * 1024 * 1024
MXU_COLS = 256
LOG2_E = 1.4426950408889634

IN_PROJ_ROWS = 1024
IN_PROJ_DOT_COLS = 512
MIX_ROWS_GMLP = 512
MIX_ROWS_ATTN = 1024
ATTN_Q_ROWS = 1024
ATTN_CHAIN_ROWS = 256
FFN_ROWS = 1024
FFN_COLS = 512
NORM_CHUNK_ROWS = 256

BF16 = jnp.bfloat16
F32 = jnp.float32


def _rms_scale(x):
    return lax.rsqrt(jnp.mean(x * x, axis=-1, keepdims=True) + EPS)


def _gelu_exact(x):
    return 0.5 * x * (1.0 + lax.erf(x * (2.0 ** -0.5)))


def _dot(a, b):
    return jnp.dot(a, b, preferred_element_type=F32)


def _params(*semantics):
    return pltpu.CompilerParams(dimension_semantics=semantics, vmem_limit_bytes=VMEM_LIMIT_BYTES)


BF16_TILE_ROWS = 16
LANES = 128


def _weight_cast_specs(w, layer, col_tile, n_steps, step_id):
    _, rows, cols = w.shape
    n_col = 1 if rows % (BF16_TILE_ROWS * n_steps) == 0 else 2
    n_row = n_steps // n_col
    assert n_row * n_col == n_steps and rows % (BF16_TILE_ROWS * n_row) == 0 and cols % (LANES * n_col) == 0
    block = (rows // n_row, cols // n_col)
    src = pl.BlockSpec((None,) + block, lambda *g: (layer, step_id(*g) // n_col, step_id(*g) % n_col))
    if col_tile is None:
        dst = pl.BlockSpec(block, lambda *g: (step_id(*g) // n_col, step_id(*g) % n_col))
        return src, dst, jax.ShapeDtypeStruct((rows, cols), BF16)
    assert n_col == 1 and cols % col_tile == 0
    dst = pl.BlockSpec((cols // col_tile, block[0], col_tile), lambda *g: (0, step_id(*g), 0))
    return src, dst, jax.ShapeDtypeStruct((cols // col_tile, rows, col_tile), BF16)


def _with_weight_casts(body, n_in, n_out, n_cast):
    def kernel(*refs):
        ins, refs = refs[:n_in], refs[n_in:]
        srcs, refs = refs[:n_cast], refs[n_cast:]
        outs, refs = refs[:n_out], refs[n_out:]
        dsts, scratch = refs[:n_cast], refs[n_cast:]
        for src, dst in zip(srcs, dsts):
            if len(dst.shape) == 2:
                dst[...] = src[...].astype(BF16)
            else:
                width = dst.shape[2]
                for t in range(dst.shape[0]):
                    dst[t] = src[:, t * width:(t + 1) * width].astype(BF16)
        body(*ins, *outs, *scratch)
    return kernel


def _mem_kv_kernel(mem_ref, g_mem_ref, w_ref, g_mk_ref, kt_ref, v_ref):
    m = mem_ref[...]
    h = (m * _rms_scale(m) * g_mem_ref[...]).astype(BF16)
    kv = _dot(h, w_ref[...])
    for hd in range(MEM_HEADS):
        k = kv[:, hd * HEAD_DIM:(hd + 1) * HEAD_DIM]
        k = k * _rms_scale(k) * g_mk_ref[...]
        kt_ref[hd * HEAD_DIM:(hd + 1) * HEAD_DIM, :] = k.T.astype(BF16)
    v_ref[...] = kv[:, MEM_WIDTH:].astype(BF16)


def _mem_kv(mem, g_mem, w_mem_kv, g_mk):
    batch = mem.shape[0]
    return pl.pallas_call(
        _mem_kv_kernel,
        grid=(DEPTH, batch),
        in_specs=[
            pl.BlockSpec((None, N_MEM, D_MODEL), lambda l, b: (b, 0, 0)),
            pl.BlockSpec((None, 1, D_MODEL), lambda l, b: (l, 0, 0)),
            pl.BlockSpec((None, D_MODEL, 2 * MEM_WIDTH), lambda l, b: (l, 0, 0)),
            pl.BlockSpec((None, 1, HEAD_DIM), lambda l, b: (l, 0, 0)),
        ],
        out_specs=[
            pl.BlockSpec((None, None, MEM_WIDTH, N_MEM), lambda l, b: (l, b, 0, 0)),
            pl.BlockSpec((None, None, N_MEM, MEM_WIDTH), lambda l, b: (l, b, 0, 0)),
        ],
        out_shape=[
            jax.ShapeDtypeStruct((DEPTH, batch, MEM_WIDTH, N_MEM), BF16),
            jax.ShapeDtypeStruct((DEPTH, batch, N_MEM, MEM_WIDTH), BF16),
        ],
        compiler_params=_params("arbitrary", "arbitrary"),
        name="mem_kv",
    )(mem, g_mem.reshape(DEPTH, 1, D_MODEL), w_mem_kv, g_mk.reshape(DEPTH, 1, HEAD_DIM))


def _norm_row_chunks(x_ref, g_ref, h_ref):
    for r in range(0, x_ref.shape[0], NORM_CHUNK_ROWS):
        rows = slice(r, r + NORM_CHUNK_ROWS)
        x = x_ref[rows, :]
        h = (x * _rms_scale(x) * g_ref[...]).astype(BF16)
        h_ref[rows, :] = h
        yield rows, h


def _project_column_chunks(x_ref, g_ref, h_ref, n_cols, project):
    for rows, h in _norm_row_chunks(x_ref, g_ref, h_ref):
        project(h, rows, 0)
    h = h_ref[...]
    for c0 in range(IN_PROJ_DOT_COLS, n_cols, IN_PROJ_DOT_COLS):
        project(h, slice(0, x_ref.shape[0]), c0)


def _in_proj_gmlp_kernel(x_ref, g_ref, w_ref, z_ref, h_ref, *, gelu_cols):
    def project(h, rows, c0):
        cols = slice(c0, c0 + IN_PROJ_DOT_COLS)
        acc = _dot(h, w_ref[:, cols])
        z_ref[rows, cols] = (_gelu_exact(acc) if c0 < gelu_cols else acc).astype(BF16)

    _project_column_chunks(x_ref, g_ref, h_ref, z_ref.shape[1], project)


def _in_proj_attn_kernel(x_ref, g_ref, w_ref, gain_ref, cos_ref, sin_ref, seg_ref, z_ref, h_ref, *, rope_cols):
    def project(h, rows, c0):
        acc = _dot(h, w_ref[:, c0:c0 + IN_PROJ_DOT_COLS])
        if c0 >= rope_cols:
            z_ref[rows, c0:c0 + IN_PROJ_DOT_COLS] = acc.astype(BF16)
            return
        cos, sin = cos_ref[rows, :], sin_ref[rows, :]
        for p0 in range(0, IN_PROJ_DOT_COLS, 2 * HEAD_DIM):
            t0, t1 = acc[:, p0:p0 + HEAD_DIM], acc[:, p0 + HEAD_DIM:p0 + 2 * HEAD_DIM]
            lanes0 = slice(c0 + p0, c0 + p0 + HEAD_DIM)
            lanes1 = slice(c0 + p0 + HEAD_DIM, c0 + p0 + 2 * HEAD_DIM)
            ss = t0 * t0 + t1 * t1
            hi = ss.astype(BF16)
            lo = (ss - hi.astype(F32)).astype(BF16)
            tot = _dot(jnp.concatenate([hi, lo], axis=1), seg_ref[...])
            r = lax.rsqrt(tot * (1.0 / HEAD_DIM) + EPS)
            n0 = t0 * r * gain_ref[:, lanes0]
            n1 = t1 * r * gain_ref[:, lanes1]
            z_ref[rows, lanes0] = (n0 * cos - n1 * sin).astype(BF16)
            z_ref[rows, lanes1] = (n1 * cos + n0 * sin).astype(BF16)

    _project_column_chunks(x_ref, g_ref, h_ref, z_ref.shape[1], project)


def _in_proj(x2d, g, w, layer, *, attn=None):
    rows, _ = x2d.shape
    cols = w.shape[2]
    bm = IN_PROJ_ROWS
    assert cols % IN_PROJ_DOT_COLS == 0
    resident = pl.Buffered(1)
    in_specs = [
        pl.BlockSpec((bm, D_MODEL), lambda i: (i, 0)),
        pl.BlockSpec((1, D_MODEL), lambda i: (0, 0)),
        pl.BlockSpec((None, D_MODEL, cols), lambda i: (layer, 0, 0), pipeline_mode=resident),
    ]
    args = [x2d, g.reshape(1, D_MODEL), w]
    if attn is None:
        assert (2 * TOK_WIDTH) % IN_PROJ_DOT_COLS == 0
        body = functools.partial(_in_proj_gmlp_kernel, gelu_cols=2 * TOK_WIDTH)
    else:
        gain, cos, sin, seg = attn
        seq_tiles = cos.shape[0] // bm
        assert (TOK_WIDTH + KV_WIDTH) % IN_PROJ_DOT_COLS == 0
        body = functools.partial(_in_proj_attn_kernel, rope_cols=TOK_WIDTH + KV_WIDTH)
        in_specs += [
            pl.BlockSpec((1, cols), lambda i: (0, 0)),
            pl.BlockSpec((bm, HEAD_DIM), lambda i: (i % seq_tiles, 0)),
            pl.BlockSpec((bm, HEAD_DIM), lambda i: (i % seq_tiles, 0)),
            pl.BlockSpec((2 * HEAD_DIM, HEAD_DIM), lambda i: (0, 0)),
        ]
        args += [gain, cos, sin, seg]
    return pl.pallas_call(
        body,
        grid=(rows // bm,),
        in_specs=in_specs,
        out_specs=pl.BlockSpec((bm, cols), lambda i: (i, 0)),
        out_shape=jax.ShapeDtypeStruct((rows, cols), BF16),
        scratch_shapes=[pltpu.VMEM((bm, D_MODEL), BF16)],
        compiler_params=_params("parallel"),
        name="in_proj_gmlp" if attn is None else "in_proj_attn",
    )(*args)


_NT_DIMS = (((1,), (1,)), ((), ()))


def _attention_kernel(q_ref, k_ref, v_ref, o_ref, v1_ref):
    @pl.when(pl.program_id(2) == 0)
    def _():
        ones_col = lax.broadcasted_iota(jnp.int32, (v_ref.shape[0], HEAD_DIM), 1) == 0
        v1_ref[:, :HEAD_DIM] = v_ref[...]
        v1_ref[:, HEAD_DIM:] = jnp.where(ones_col, 1.0, 0.0).astype(BF16)

    k = k_ref[...]
    v = v1_ref[...]
    half = HEAD_DIM // 2
    lane = lax.broadcasted_iota(jnp.int32, (ATTN_CHAIN_ROWS, 2 * HEAD_DIM), 1) % HEAD_DIM
    first_lane = (pl.program_id(1) % 2) * half
    keep = jnp.logical_and(lane >= first_lane, lane < first_lane + half)
    for r in range(0, q_ref.shape[0], ATTN_CHAIN_ROWS):
        rows = slice(r, r + ATTN_CHAIN_ROWS)
        for g in range(Q_PER_KV):
            q = q_ref[rows, g * 2 * HEAD_DIM:(g + 1) * 2 * HEAD_DIM]
            q = jnp.where(keep, q, jnp.zeros_like(q))
            s = lax.dot_general(q, k, _NT_DIMS, preferred_element_type=F32)
            p = jnp.exp2((s - jnp.max(s, axis=-1, keepdims=True)).astype(BF16))
            o = _dot(p, v)
            o = o[:, :HEAD_DIM] / o[:, HEAD_DIM:HEAD_DIM + 1]
            o_ref[rows, g * HEAD_DIM:(g + 1) * HEAD_DIM] = o.astype(BF16)


def _attention(z, batch, seq, casts=()):
    bq = ATTN_Q_ROWS
    q_tiles = seq // bq
    pair_width = 2 * HEAD_DIM
    k_block0 = TOK_WIDTH // pair_width
    v_block0 = (TOK_WIDTH + KV_WIDTH) // HEAD_DIM
    grid = (batch, KV_HEADS, q_tiles)
    cast_specs = [_weight_cast_specs(*cast, batch * KV_HEADS * q_tiles,
                                     lambda b, h, i: (b * KV_HEADS + h) * q_tiles + i) for cast in casts]
    in_specs = [
        pl.BlockSpec((bq, Q_PER_KV * pair_width), lambda b, h, i: (b * q_tiles + i, h // 2)),
        pl.BlockSpec((seq, pair_width), lambda b, h, i: (b, k_block0 + h // 2)),
        pl.BlockSpec((seq, HEAD_DIM), lambda b, h, i: (b, v_block0 + h)),
    ]
    return pl.pallas_call(
        _with_weight_casts(_attention_kernel, len(in_specs), 1, len(casts)),
        grid=grid,
        in_specs=in_specs + [src for src, _, _ in cast_specs],
        out_specs=[pl.BlockSpec((bq, Q_PER_KV * HEAD_DIM), lambda b, h, i: (b * q_tiles + i, h))]
        + [dst for _, dst, _ in cast_specs],
        out_shape=[jax.ShapeDtypeStruct((batch * seq, TOK_WIDTH), BF16)] + [shape for _, _, shape in cast_specs],
        scratch_shapes=[pltpu.VMEM((seq, 2 * HEAD_DIM), BF16)],
        compiler_params=_params("parallel", "parallel", "arbitrary"),
        name="attention",
    )(z, z, z, *[cast[0] for cast in casts])


def _memory_attention_into(mixed_ref, qm_ref, g_mq_ref, kt_ref, v_ref):
    for hd in range(MEM_HEADS):
        lanes = slice(hd * HEAD_DIM, (hd + 1) * HEAD_DIM)
        q = qm_ref[:, lanes].astype(F32)
        q = (q * _rms_scale(q) * g_mq_ref[...] * SCORE_SCALE).astype(BF16)
        s = _dot(q, kt_ref[lanes, :])
        p = jnp.exp(s - jnp.max(s, axis=-1, keepdims=True))
        denom = jnp.sum(p, axis=-1, keepdims=True)
        o = _dot(p.astype(BF16), v_ref[:, lanes]) / denom
        mixed_ref[:, TOK_WIDTH + hd * HEAD_DIM:TOK_WIDTH + (hd + 1) * HEAD_DIM] = o.astype(BF16)


def _mix_out_gmlp_kernel(u_ref, v_ref, qm_ref, g_v_ref, ws_ref, bs_ref, g_mq_ref, kt_ref, mv_ref,
                         w_out_ref, x_ref, o_ref, mixed_ref):
    v = v_ref[...].astype(F32)
    vn = (v * _rms_scale(v) * g_v_ref[...]).astype(BF16)
    for c in range(v.shape[0] // CHUNK):
        rows = slice(c * CHUNK, (c + 1) * CHUNK)
        for g in range(A_GROUPS):
            lanes = slice(g * HEAD_DIM, (g + 1) * HEAD_DIM)
            s = _dot(ws_ref[g], vn[rows, lanes]) + bs_ref[g]
            mixed_ref[rows, lanes] = (u_ref[rows, lanes].astype(F32) * s).astype(BF16)
    _memory_attention_into(mixed_ref, qm_ref, g_mq_ref, kt_ref, mv_ref)
    o_ref[...] = x_ref[...] + _dot(mixed_ref[...], w_out_ref[...])


def _mix_out_attn_kernel(t_ref, qm_ref, g_mq_ref, kt_ref, mv_ref, w_out_ref, x_ref, o_ref, mixed_ref):
    mixed_ref[:, :TOK_WIDTH] = t_ref[...]
    _memory_attention_into(mixed_ref, qm_ref, g_mq_ref, kt_ref, mv_ref)
    o_ref[...] = x_ref[...] + _dot(mixed_ref[...], w_out_ref[...])


def _mix_out(x2d, z, mem_kt, mem_v, g_mq, w_out, layer, seq, *, gmlp=None, tok=None, casts=()):
    rows = x2d.shape[0]
    bm = MIX_ROWS_GMLP if gmlp is not None else MIX_ROWS_ATTN
    seq_tiles = seq // bm
    qm_col = (z.shape[1] - MEM_WIDTH) // MEM_WIDTH
    shared_specs = [
        pl.BlockSpec((bm, MEM_WIDTH), lambda i: (i, qm_col)),
    ]
    tail_specs = [
        pl.BlockSpec((1, HEAD_DIM), lambda i: (0, 0)),
        pl.BlockSpec((None, None, MEM_WIDTH, N_MEM), lambda i: (layer, i // seq_tiles, 0, 0)),
        pl.BlockSpec((None, None, N_MEM, MEM_WIDTH), lambda i: (layer, i // seq_tiles, 0, 0)),
        pl.BlockSpec((None, D_MODEL, D_MODEL), lambda i: (layer, 0, 0), pipeline_mode=pl.Buffered(1)),
        pl.BlockSpec((bm, D_MODEL), lambda i: (i, 0)),
    ]
    tail_args = [g_mq.reshape(1, HEAD_DIM), mem_kt, mem_v, w_out, x2d]
    if gmlp is not None:
        g_v, w_s, b_s, gmlp_layer = gmlp
        body = _mix_out_gmlp_kernel
        in_specs = [
            pl.BlockSpec((bm, TOK_WIDTH), lambda i: (i, 0)),
            pl.BlockSpec((bm, TOK_WIDTH), lambda i: (i, 1)),
        ] + shared_specs + [
            pl.BlockSpec((1, TOK_WIDTH), lambda i: (0, 0)),
            pl.BlockSpec((None, A_GROUPS, CHUNK, CHUNK), lambda i: (gmlp_layer, 0, 0, 0)),
            pl.BlockSpec((A_GROUPS, CHUNK, 1), lambda i: (0, 0, 0)),
        ] + tail_specs
        args = [z, z, z, g_v.reshape(1, TOK_WIDTH), w_s, b_s.reshape(A_GROUPS, CHUNK, 1)] + tail_args
        name = "mix_out_gmlp"
    else:
        body = _mix_out_attn_kernel
        in_specs = [pl.BlockSpec((bm, TOK_WIDTH), lambda i: (i, 0))] + shared_specs + tail_specs
        args = [tok, z] + tail_args
        name = "mix_out_attn"
    cast_specs = [_weight_cast_specs(*cast, rows // bm, lambda i: i) for cast in casts]
    return pl.pallas_call(
        _with_weight_casts(body, len(in_specs), 1, len(casts)),
        grid=(rows // bm,),
        in_specs=in_specs + [src for src, _, _ in cast_specs],
        out_specs=[pl.BlockSpec((bm, D_MODEL), lambda i: (i, 0))] + [dst for _, dst, _ in cast_specs],
        out_shape=[jax.ShapeDtypeStruct((rows, D_MODEL), F32)] + [shape for _, _, shape in cast_specs],
        scratch_shapes=[pltpu.VMEM((bm, D_MODEL), BF16)],
        compiler_params=_params("parallel"),
        name=name,
    )(*args, *[cast[0] for cast in casts])


def _ffn_kernel(x_ref, g_ref, wg_ref, wu_ref, wd_ref, o_ref, h_ref):
    j = pl.program_id(1)

    def contribution(h):
        gate = _dot(h, wg_ref[...])
        up = _dot(h, wu_ref[...])
        act = (jax.nn.silu(gate) * up).astype(BF16)
        return _dot(act, wd_ref[...])

    @pl.when(j == 0)
    def _():
        for rows, h in _norm_row_chunks(x_ref, g_ref, h_ref):
            o_ref[rows, :] = x_ref[rows, :] + contribution(h)

    @pl.when(j > 0)
    def _():
        o_ref[...] += contribution(h_ref[...])


def _ffn(x2d, g, w_gate_up, w_down):
    rows = x2d.shape[0]
    d_ff = w_down.shape[0]
    bm, bf = FFN_ROWS, FFN_COLS
    ff_tiles = d_ff // bf
    return pl.pallas_call(
        _ffn_kernel,
        grid=(rows // bm, ff_tiles),
        in_specs=[
            pl.BlockSpec((bm, D_MODEL), lambda i, j: (i, 0)),
            pl.BlockSpec((1, D_MODEL), lambda i, j: (0, 0)),
            pl.BlockSpec((None, D_MODEL, bf), lambda i, j: (j, 0, 0)),
            pl.BlockSpec((None, D_MODEL, bf), lambda i, j: (ff_tiles + j, 0, 0)),
            pl.BlockSpec((bf, D_MODEL), lambda i, j: (j, 0)),
        ],
        out_specs=pl.BlockSpec((bm, D_MODEL), lambda i, j: (i, 0)),
        out_shape=jax.ShapeDtypeStruct((rows, D_MODEL), F32),
        scratch_shapes=[pltpu.VMEM((bm, D_MODEL), BF16)],
        compiler_params=_params("parallel", "arbitrary"),
        name="ffn",
    )(x2d, g.reshape(1, D_MODEL), w_gate_up, w_gate_up, w_down)


def _pair_layout_columns(a, kv_groups):
    lead = a.shape[:-1]
    n = len(lead)
    a = a.reshape(lead + (KV_HEADS // 2, 2, kv_groups, 2, 2, ROPE_PAIRS))
    a = a.transpose(tuple(range(n)) + (n, n + 2, n + 4, n + 1, n + 3, n + 5))
    return a.reshape(lead + (KV_HEADS * kv_groups * HEAD_DIM,))


def _pair_layout_weights(w):
    q, k, rest = w[..., :TOK_WIDTH], w[..., TOK_WIDTH:TOK_WIDTH + KV_WIDTH], w[..., TOK_WIDTH + KV_WIDTH:]
    return jnp.concatenate([_pair_layout_columns(q, Q_PER_KV), _pair_layout_columns(k, 1), rest], axis=-1)


def _attn_tables(seq):
    n_rows = seq // GRID_W
    rows = jnp.broadcast_to(jnp.arange(n_rows)[:, None], (n_rows, GRID_W)).reshape(seq)
    cols = jnp.broadcast_to(jnp.arange(GRID_W)[None, :], (n_rows, GRID_W)).reshape(seq)
    freqs = ROPE_THETA ** (-jnp.arange(ROPE_PAIRS, dtype=F32) / ROPE_PAIRS)
    ang_r = rows.astype(F32)[:, None] * freqs
    ang_c = cols.astype(F32)[:, None] * freqs
    ang = jnp.concatenate([ang_r, ang_c, ang_r, ang_c], axis=-1)
    k_seg = (jnp.arange(2 * HEAD_DIM) % HEAD_DIM) // (HEAD_DIM // 2)
    n_seg = jnp.arange(HEAD_DIM) // (HEAD_DIM // 2)
    seg = (k_seg[:, None] == n_seg[None, :]).astype(BF16)
    return jnp.cos(ang), jnp.sin(ang), seg


def _attn_gains(g_q, g_k, cols):
    gq = _pair_layout_columns(jnp.tile(g_q * (SCORE_SCALE * LOG2_E), Q_HEADS), Q_PER_KV)
    gk = _pair_layout_columns(jnp.tile(g_k, KV_HEADS), 1)
    return jnp.concatenate([gq, gk, jnp.ones((cols - TOK_WIDTH - KV_WIDTH,), F32)]).reshape(1, cols)


def kernel(x, mem, g_mix, g_ffn, w_in_a, g_v_a, w_spatial, b_spatial, w_in_b, g_q_b, g_k_b, g_mem, w_mem_kv,
           g_mq, g_mk, w_out, w_gate_up, w_down):
    batch, seq, _ = x.shape
    rows = batch * seq
    assert seq % IN_PROJ_ROWS == 0 and seq % ATTN_Q_ROWS == 0 and rows % FFN_ROWS == 0
    assert seq % MIX_ROWS_GMLP == 0 and seq % MIX_ROWS_ATTN == 0 and MIX_ROWS_GMLP % CHUNK == 0
    assert w_down.shape[1] % FFN_COLS == 0

    cos, sin, seg = _attn_tables(seq)
    mem_kt, mem_v = _mem_kv(mem, g_mem, w_mem_kv.astype(BF16), g_mk)
    w_in_a, w_spatial, w_out = w_in_a.astype(BF16), w_spatial.astype(BF16), w_out.astype(BF16)
    w_in_b = _pair_layout_weights(w_in_b.astype(BF16))
    xs = x.reshape(rows, D_MODEL)
    for l in range(DEPTH):
        idx = l // 2
        ffn_casts = ((w_gate_up, l, FFN_COLS), (w_down, l, None))
        if l % 2 == 0:
            z = _in_proj(xs, g_mix[l], w_in_a, idx)
            xs, ffn_gate_up, ffn_down = _mix_out(xs, z, mem_kt, mem_v, g_mq[l], w_out, l, seq,
                                                 gmlp=(g_v_a[idx], w_spatial, b_spatial[idx], idx),
                                                 casts=ffn_casts)
        else:
            gain = _attn_gains(g_q_b[idx], g_k_b[idx], w_in_b.shape[2])
            z = _in_proj(xs, g_mix[l], w_in_b, idx, attn=(gain, cos, sin, seg))
            tok, ffn_gate_up, ffn_down = _attention(z, batch, seq, casts=ffn_casts)
            xs, = _mix_out(xs, z, mem_kt, mem_v, g_mq[l], w_out, l, seq, tok=tok)
        xs = _ffn(xs, g_ffn[l], ffn_gate_up, ffn_down)
    return xs.reshape(batch, seq, D_MODEL)
```

```python
import functools

import jax
import jax.numpy as jnp
from jax import lax
from jax.experimental import pallas as pl
from jax.experimental.pallas import tpu as pltpu

D_MODEL = 2048
DEPTH = 4
N_MEM = 256
GRID_W = 64
HEAD_DIM = 128
MEM_HEADS = 4
MEM_WIDTH = MEM_HEADS * HEAD_DIM
TOK_WIDTH = D_MODEL - MEM_WIDTH
CHUNK = 128
A_GROUPS = TOK_WIDTH // HEAD_DIM
Q_HEADS = TOK_WIDTH // HEAD_DIM
KV_HEADS = 4
Q_PER_KV = Q_HEADS // KV_HEADS
KV_WIDTH = KV_HEADS * HEAD_DIM
ROPE_THETA = 10000.0
ROPE_PAIRS = HEAD_DIM // 4
EPS = 1e-6
SCORE_SCALE = HEAD_DIM ** -0.5

VMEM_LIMIT_BYTES = 63 * 1024 * 1024
MXU_COLS = 256
LOG2_E = 1.4426950408889634

IN_PROJ_ROWS = 1024
IN_PROJ_DOT_COLS = 512
MIX_ROWS_GMLP = 512
MIX_ROWS_ATTN = 1024
ATTN_Q_ROWS = 1024
ATTN_CHAIN_ROWS = 256
FFN_ROWS = 1024
FFN_COLS = 512
NORM_CHUNK_ROWS = 256

BF16 = jnp.bfloat16
F32 = jnp.float32


def _rms_scale(x):
    return lax.rsqrt(jnp.mean(x * x, axis=-1, keepdims=True) + EPS)


def _gelu_exact(x):
    return 0.5 * x * (1.0 + lax.erf(x * (2.0 ** -0.5)))


def _dot(a, b):
    return jnp.dot(a, b, preferred_element_type=F32)


def _params(*semantics):
    return pltpu.CompilerParams(dimension_semantics=semantics, vmem_limit_bytes=VMEM_LIMIT_BYTES)


BF16_TILE_ROWS = 16
LANES = 128


def _weight_cast_specs(w, layer, n_steps, step_id):
    _, rows, cols = w.shape
    n_col = 1 if rows % (BF16_TILE_ROWS * n_steps) == 0 else 2
    n_row = n_steps // n_col
    assert n_row * n_col == n_steps and rows % (BF16_TILE_ROWS * n_row) == 0 and cols % (LANES * n_col) == 0
    block = (rows // n_row, cols // n_col)
    src = pl.BlockSpec((None,) + block, lambda *g: (layer, step_id(*g) // n_col, step_id(*g) % n_col))
    dst = pl.BlockSpec(block, lambda *g: (step_id(*g) // n_col, step_id(*g) % n_col))
    return src, dst, jax.ShapeDtypeStruct((rows, cols), BF16)


def _with_weight_casts(body, n_in, n_out, n_cast):
    def kernel(*refs):
        ins, refs = refs[:n_in], refs[n_in:]
        srcs, refs = refs[:n_cast], refs[n_cast:]
        outs, refs = refs[:n_out], refs[n_out:]
        dsts, scratch = refs[:n_cast], refs[n_cast:]
        for src, dst in zip(srcs, dsts):
            dst[...] = src[...].astype(BF16)
        body(*ins, *outs, *scratch)
    return kernel


def _mem_kv_kernel(mem_ref, g_mem_ref, w_ref, g_mk_ref, kt_ref, v_ref):
    m = mem_ref[...]
    h = (m * _rms_scale(m) * g_mem_ref[...]).astype(BF16)
    kv = _dot(h, w_ref[...])
    for hd in range(MEM_HEADS):
        k = kv[:, hd * HEAD_DIM:(hd + 1) * HEAD_DIM]
        k = k * _rms_scale(k) * g_mk_ref[...]
        kt_ref[hd * HEAD_DIM:(hd + 1) * HEAD_DIM, :] = k.T.astype(BF16)
    v_ref[...] = kv[:, MEM_WIDTH:].astype(BF16)


def _mem_kv(mem, g_mem, w_mem_kv, g_mk):
    batch = mem.shape[0]
    return pl.pallas_call(
        _mem_kv_kernel,
        grid=(DEPTH, batch),
        in_specs=[
            pl.BlockSpec((None, N_MEM, D_MODEL), lambda l, b: (b, 0, 0)),
            pl.BlockSpec((None, 1, D_MODEL), lambda l, b: (l, 0, 0)),
            pl.BlockSpec((None, D_MODEL, 2 * MEM_WIDTH), lambda l, b: (l, 0, 0)),
            pl.BlockSpec((None, 1, HEAD_DIM), lambda l, b: (l, 0, 0)),
        ],
        out_specs=[
            pl.BlockSpec((None, None, MEM_WIDTH, N_MEM), lambda l, b: (l, b, 0, 0)),
            pl.BlockSpec((None, None, N_MEM, MEM_WIDTH), lambda l, b: (l, b, 0, 0)),
        ],
        out_shape=[
            jax.ShapeDtypeStruct((DEPTH, batch, MEM_WIDTH, N_MEM), BF16),
            jax.ShapeDtypeStruct((DEPTH, batch, N_MEM, MEM_WIDTH), BF16),
        ],
        compiler_params=_params("arbitrary", "arbitrary"),
        name="mem_kv",
    )(mem, g_mem.reshape(DEPTH, 1, D_MODEL), w_mem_kv, g_mk.reshape(DEPTH, 1, HEAD_DIM))


def _norm_row_chunks(x_ref, g_ref, h_ref):
    for r in range(0, x_ref.shape[0], NORM_CHUNK_ROWS):
        rows = slice(r, r + NORM_CHUNK_ROWS)
        x = x_ref[rows, :]
        h = (x * _rms_scale(x) * g_ref[...]).astype(BF16)
        h_ref[rows, :] = h
        yield rows, h


def _project_column_chunks(x_ref, g_ref, h_ref, n_cols, project):
    for rows, h in _norm_row_chunks(x_ref, g_ref, h_ref):
        project(h, rows, 0)
    h = h_ref[...]
    for c0 in range(IN_PROJ_DOT_COLS, n_cols, IN_PROJ_DOT_COLS):
        project(h, slice(0, x_ref.shape[0]), c0)


def _in_proj_gmlp_kernel(x_ref, g_ref, w_ref, z_ref, h_ref, *, gelu_cols):
    def project(h, rows, c0):
        cols = slice(c0, c0 + IN_PROJ_DOT_COLS)
        acc = _dot(h, w_ref[:, cols])
        z_ref[rows, cols] = (_gelu_exact(acc) if c0 < gelu_cols else acc).astype(BF16)

    _project_column_chunks(x_ref, g_ref, h_ref, z_ref.shape[1], project)


def _in_proj_attn_kernel(x_ref, g_ref, w_ref, gain_ref, cos_ref, sin_ref, seg_ref, z_ref, h_ref, *, rope_cols):
    def project(h, rows, c0):
        acc = _dot(h, w_ref[:, c0:c0 + IN_PROJ_DOT_COLS])
        if c0 >= rope_cols:
            z_ref[rows, c0:c0 + IN_PROJ_DOT_COLS] = acc.astype(BF16)
            return
        cos, sin = cos_ref[rows, :], sin_ref[rows, :]
        for p0 in range(0, IN_PROJ_DOT_COLS, 2 * HEAD_DIM):
            t0, t1 = acc[:, p0:p0 + HEAD_DIM], acc[:, p0 + HEAD_DIM:p0 + 2 * HEAD_DIM]
            lanes0 = slice(c0 + p0, c0 + p0 + HEAD_DIM)
            lanes1 = slice(c0 + p0 + HEAD_DIM, c0 + p0 + 2 * HEAD_DIM)
            ss = t0 * t0 + t1 * t1
            hi = ss.astype(BF16)
            lo = (ss - hi.astype(F32)).astype(BF16)
            tot = _dot(jnp.concatenate([hi, lo], axis=1), seg_ref[...])
            r = lax.rsqrt(tot * (1.0 / HEAD_DIM) + EPS)
            n0 = t0 * r * gain_ref[:, lanes0]
            n1 = t1 * r * gain_ref[:, lanes1]
            z_ref[rows, lanes0] = (n0 * cos - n1 * sin).astype(BF16)
            z_ref[rows, lanes1] = (n1 * cos + n0 * sin).astype(BF16)

    _project_column_chunks(x_ref, g_ref, h_ref, z_ref.shape[1], project)


def _in_proj(x2d, g, w, layer, *, attn=None, casts=()):
    rows, _ = x2d.shape
    cols = w.shape[2]
    bm = IN_PROJ_ROWS
    assert cols % IN_PROJ_DOT_COLS == 0
    resident = pl.Buffered(1)
    in_specs = [
        pl.BlockSpec((bm, D_MODEL), lambda i: (i, 0)),
        pl.BlockSpec((1, D_MODEL), lambda i: (0, 0)),
        pl.BlockSpec((None, D_MODEL, cols), lambda i: (layer, 0, 0), pipeline_mode=resident),
    ]
    args = [x2d, g.reshape(1, D_MODEL), w]
    if attn is None:
        assert (2 * TOK_WIDTH) % IN_PROJ_DOT_COLS == 0
        body = functools.partial(_in_proj_gmlp_kernel, gelu_cols=2 * TOK_WIDTH)
    else:
        gain, cos, sin, seg = attn
        seq_tiles = cos.shape[0] // bm
        assert (TOK_WIDTH + KV_WIDTH) % IN_PROJ_DOT_COLS == 0
        body = functools.partial(_in_proj_attn_kernel, rope_cols=TOK_WIDTH + KV_WIDTH)
        in_specs += [
            pl.BlockSpec((1, cols), lambda i: (0, 0)),
            pl.BlockSpec((bm, HEAD_DIM), lambda i: (i % seq_tiles, 0)),
            pl.BlockSpec((bm, HEAD_DIM), lambda i: (i % seq_tiles, 0)),
            pl.BlockSpec((2 * HEAD_DIM, HEAD_DIM), lambda i: (0, 0)),
        ]
        args += [gain, cos, sin, seg]
    cast_specs = [_weight_cast_specs(*cast, rows // bm, lambda i: i) for cast in casts]
    return pl.pallas_call(
        _with_weight_casts(body, len(in_specs), 1, len(casts)),
        grid=(rows // bm,),
        in_specs=in_specs + [src for src, _, _ in cast_specs],
        out_specs=[pl.BlockSpec((bm, cols), lambda i: (i, 0))] + [dst for _, dst, _ in cast_specs],
        out_shape=[jax.ShapeDtypeStruct((rows, cols), BF16)] + [shape for _, _, shape in cast_specs],
        scratch_shapes=[pltpu.VMEM((bm, D_MODEL), BF16)],
        compiler_params=_params("parallel"),
        name="in_proj_gmlp" if attn is None else "in_proj_attn",
    )(*args, *[cast[0] for cast in casts])


_NT_DIMS = (((1,), (1,)), ((), ()))


def _attention_kernel(q_ref, k_ref, v_ref, o_ref, v1_ref):
    @pl.when(pl.program_id(2) == 0)
    def _():
        ones_col = lax.broadcasted_iota(jnp.int32, (v_ref.shape[0], HEAD_DIM), 1) == 0
        v1_ref[:, :HEAD_DIM] = v_ref[...]
        v1_ref[:, HEAD_DIM:] = jnp.where(ones_col, 1.0, 0.0).astype(BF16)

    k = k_ref[...]
    v = v1_ref[...]
    half = HEAD_DIM // 2
    lane = lax.broadcasted_iota(jnp.int32, (ATTN_CHAIN_ROWS, 2 * HEAD_DIM), 1) % HEAD_DIM
    first_lane = (pl.program_id(1) % 2) * half
    keep = jnp.logical_and(lane >= first_lane, lane < first_lane + half)
    for r in range(0, q_ref.shape[0], ATTN_CHAIN_ROWS):
        rows = slice(r, r + ATTN_CHAIN_ROWS)
        for g in range(Q_PER_KV):
            q = q_ref[rows, g * 2 * HEAD_DIM:(g + 1) * 2 * HEAD_DIM]
            q = jnp.where(keep, q, jnp.zeros_like(q))
            s = lax.dot_general(q, k, _NT_DIMS, preferred_element_type=F32)
            p = jnp.exp2((s - jnp.max(s, axis=-1, keepdims=True)).astype(BF16))
            o = _dot(p, v)
            o = o[:, :HEAD_DIM] / o[:, HEAD_DIM:HEAD_DIM + 1]
            o_ref[rows, g * HEAD_DIM:(g + 1) * HEAD_DIM] = o.astype(BF16)


def _attention(z, batch, seq, casts=()):
    bq = ATTN_Q_ROWS
    q_tiles = seq // bq
    pair_width = 2 * HEAD_DIM
    k_block0 = TOK_WIDTH // pair_width
    v_block0 = (TOK_WIDTH + KV_WIDTH) // HEAD_DIM
    grid = (batch, KV_HEADS, q_tiles)
    cast_specs = [_weight_cast_specs(*cast, batch * KV_HEADS * q_tiles,
                                     lambda b, h, i: (b * KV_HEADS + h) * q_tiles + i) for cast in casts]
    in_specs = [
        pl.BlockSpec((bq, Q_PER_KV * pair_width), lambda b, h, i: (b * q_tiles + i, h // 2)),
        pl.BlockSpec((seq, pair_width), lambda b, h, i: (b, k_block0 + h // 2)),
        pl.BlockSpec((seq, HEAD_DIM), lambda b, h, i: (b, v_block0 + h)),
    ]
    return pl.pallas_call(
        _with_weight_casts(_attention_kernel, len(in_specs), 1, len(casts)),
        grid=grid,
        in_specs=in_specs + [src for src, _, _ in cast_specs],
        out_specs=[pl.BlockSpec((bq, Q_PER_KV * HEAD_DIM), lambda b, h, i: (b * q_tiles + i, h))]
        + [dst for _, dst, _ in cast_specs],
        out_shape=[jax.ShapeDtypeStruct((batch * seq, TOK_WIDTH), BF16)] + [shape for _, _, shape in cast_specs],
        scratch_shapes=[pltpu.VMEM((seq, 2 * HEAD_DIM), BF16)],
        compiler_params=_params("parallel", "parallel", "arbitrary"),
        name="attention",
    )(z, z, z, *[cast[0] for cast in casts])


def _memory_attention_into(mixed_ref, qm_ref, g_mq_ref, kt_ref, v_ref):
    for hd in range(MEM_HEADS):
        lanes = slice(hd * HEAD_DIM, (hd + 1) * HEAD_DIM)
        q = qm_ref[:, lanes].astype(F32)
        q = (q * _rms_scale(q) * g_mq_ref[...] * SCORE_SCALE).astype(BF16)
        s = _dot(q, kt_ref[lanes, :])
        p = jnp.exp(s - jnp.max(s, axis=-1, keepdims=True))
        denom = jnp.sum(p, axis=-1, keepdims=True)
        o = _dot(p.astype(BF16), v_ref[:, lanes]) / denom
        mixed_ref[:, TOK_WIDTH + hd * HEAD_DIM:TOK_WIDTH + (hd + 1) * HEAD_DIM] = o.astype(BF16)


def _mix_out_gmlp_kernel(u_ref, v_ref, qm_ref, g_v_ref, ws_ref, bs_ref, g_mq_ref, kt_ref, mv_ref,
                         w_out_ref, x_ref, o_ref, mixed_ref):
    v = v_ref[...].astype(F32)
    vn = (v * _rms_scale(v) * g_v_ref[...]).astype(BF16)
    for c in range(v.shape[0] // CHUNK):
        rows = slice(c * CHUNK, (c + 1) * CHUNK)
        for g in range(A_GROUPS):
            lanes = slice(g * HEAD_DIM, (g + 1) * HEAD_DIM)
            s = _dot(ws_ref[g], vn[rows, lanes]) + bs_ref[g]
            mixed_ref[rows, lanes] = (u_ref[rows, lanes].astype(F32) * s).astype(BF16)
    _memory_attention_into(mixed_ref, qm_ref, g_mq_ref, kt_ref, mv_ref)
    o_ref[...] = x_ref[...] + _dot(mixed_ref[...], w_out_ref[...])


def _mix_out_attn_kernel(t_ref, qm_ref, g_mq_ref, kt_ref, mv_ref, w_out_ref, x_ref, o_ref, mixed_ref):
    mixed_ref[:, :TOK_WIDTH] = t_ref[...]
    _memory_attention_into(mixed_ref, qm_ref, g_mq_ref, kt_ref, mv_ref)
    o_ref[...] = x_ref[...] + _dot(mixed_ref[...], w_out_ref[...])


def _mix_out(x2d, z, mem_kt, mem_v, g_mq, w_out, layer, seq, *, gmlp=None, tok=None, casts=()):
    rows = x2d.shape[0]
    bm = MIX_ROWS_GMLP if gmlp is not None else MIX_ROWS_ATTN
    seq_tiles = seq // bm
    qm_col = (z.shape[1] - MEM_WIDTH) // MEM_WIDTH
    shared_specs = [
        pl.BlockSpec((bm, MEM_WIDTH), lambda i: (i, qm_col)),
    ]
    tail_specs = [
        pl.BlockSpec((1, HEAD_DIM), lambda i: (0, 0)),
        pl.BlockSpec((None, None, MEM_WIDTH, N_MEM), lambda i: (layer, i // seq_tiles, 0, 0)),
        pl.BlockSpec((None, None, N_MEM, MEM_WIDTH), lambda i: (layer, i // seq_tiles, 0, 0)),
        pl.BlockSpec((D_MODEL, D_MODEL), lambda i: (0, 0), pipeline_mode=pl.Buffered(1)),
        pl.BlockSpec((bm, D_MODEL), lambda i: (i, 0)),
    ]
    tail_args = [g_mq.reshape(1, HEAD_DIM), mem_kt, mem_v, w_out, x2d]
    if gmlp is not None:
        g_v, w_s, b_s, gmlp_layer = gmlp
        body = _mix_out_gmlp_kernel
        in_specs = [
            pl.BlockSpec((bm, TOK_WIDTH), lambda i: (i, 0)),
            pl.BlockSpec((bm, TOK_WIDTH), lambda i: (i, 1)),
        ] + shared_specs + [
            pl.BlockSpec((1, TOK_WIDTH), lambda i: (0, 0)),
            pl.BlockSpec((None, A_GROUPS, CHUNK, CHUNK), lambda i: (gmlp_layer, 0, 0, 0)),
            pl.BlockSpec((A_GROUPS, CHUNK, 1), lambda i: (0, 0, 0)),
        ] + tail_specs
        args = [z, z, z, g_v.reshape(1, TOK_WIDTH), w_s, b_s.reshape(A_GROUPS, CHUNK, 1)] + tail_args
        name = "mix_out_gmlp"
    else:
        body = _mix_out_attn_kernel
        in_specs = [pl.BlockSpec((bm, TOK_WIDTH), lambda i: (i, 0))] + shared_specs + tail_specs
        args = [tok, z] + tail_args
        name = "mix_out_attn"
    cast_specs = [_weight_cast_specs(*cast, rows // bm, lambda i: i) for cast in casts]
    return pl.pallas_call(
        _with_weight_casts(body, len(in_specs), 1, len(casts)),
        grid=(rows // bm,),
        in_specs=in_specs + [src for src, _, _ in cast_specs],
        out_specs=[pl.BlockSpec((bm, D_MODEL), lambda i: (i, 0))] + [dst for _, dst, _ in cast_specs],
        out_shape=[jax.ShapeDtypeStruct((rows, D_MODEL), F32)] + [shape for _, _, shape in cast_specs],
        scratch_shapes=[pltpu.VMEM((bm, D_MODEL), BF16)],
        compiler_params=_params("parallel"),
        name=name,
    )(*args, *[cast[0] for cast in casts])


def _ffn_kernel(x_ref, g_ref, wg_ref, wu_ref, wd_ref, o_ref, h_ref):
    j = pl.program_id(1)

    def contribution(h):
        gate = _dot(h, wg_ref[...])
        up = _dot(h, wu_ref[...])
        act = (jax.nn.silu(gate) * up).astype(BF16)
        return _dot(act, wd_ref[...])

    @pl.when(j == 0)
    def _():
        for rows, h in _norm_row_chunks(x_ref, g_ref, h_ref):
            o_ref[rows, :] = x_ref[rows, :] + contribution(h)

    @pl.when(j > 0)
    def _():
        o_ref[...] += contribution(h_ref[...])


def _ffn(x2d, g, w_gate_up, w_down):
    rows = x2d.shape[0]
    d_ff = w_down.shape[0]
    bm, bf = FFN_ROWS, FFN_COLS
    ff_tiles = d_ff // bf
    return pl.pallas_call(
        _ffn_kernel,
        grid=(rows // bm, ff_tiles),
        in_specs=[
            pl.BlockSpec((bm, D_MODEL), lambda i, j: (i, 0)),
            pl.BlockSpec((1, D_MODEL), lambda i, j: (0, 0)),
            pl.BlockSpec((D_MODEL, bf), lambda i, j: (0, j)),
            pl.BlockSpec((D_MODEL, bf), lambda i, j: (0, ff_tiles + j)),
            pl.BlockSpec((bf, D_MODEL), lambda i, j: (j, 0)),
        ],
        out_specs=pl.BlockSpec((bm, D_MODEL), lambda i, j: (i, 0)),
        out_shape=jax.ShapeDtypeStruct((rows, D_MODEL), F32),
        scratch_shapes=[pltpu.VMEM((bm, D_MODEL), BF16)],
        compiler_params=_params("parallel", "arbitrary"),
        name="ffn",
    )(x2d, g.reshape(1, D_MODEL), w_gate_up, w_gate_up, w_down)


def _pair_layout_columns(a):
    lead = a.shape[:-1]
    n = len(lead)
    axes = tuple(range(n))
    half_pairs = KV_HEADS // 2
    q = a[..., :TOK_WIDTH].reshape(lead + (half_pairs, 2, Q_PER_KV, HEAD_DIM))
    q = q.transpose(axes + (n, n + 2, n + 1, n + 3)).reshape(lead + (half_pairs * Q_PER_KV, 2, HEAD_DIM))
    k = a[..., TOK_WIDTH:].reshape(lead + (half_pairs, 2, HEAD_DIM))
    blocks = jnp.concatenate([q, k], axis=n)
    blocks = blocks.reshape(lead + (blocks.shape[n], 2, 2, 2, ROPE_PAIRS))
    blocks = blocks.transpose(axes + (n, n + 3, n + 1, n + 2, n + 4))
    return blocks.reshape(lead + (TOK_WIDTH + KV_WIDTH,))


def _pair_layout_weights(w):
    rope_width = TOK_WIDTH + KV_WIDTH
    return jnp.concatenate([_pair_layout_columns(w[..., :rope_width]), w[..., rope_width:]], axis=-1)


def _attn_tables(seq):
    n_rows = seq // GRID_W
    rows = jnp.broadcast_to(jnp.arange(n_rows)[:, None], (n_rows, GRID_W)).reshape(seq)
    cols = jnp.broadcast_to(jnp.arange(GRID_W)[None, :], (n_rows, GRID_W)).reshape(seq)
    freqs = ROPE_THETA ** (-jnp.arange(ROPE_PAIRS, dtype=F32) / ROPE_PAIRS)
    ang_r = rows.astype(F32)[:, None] * freqs
    ang_c = cols.astype(F32)[:, None] * freqs
    ang = jnp.concatenate([ang_r, ang_c, ang_r, ang_c], axis=-1)
    k_seg = (jnp.arange(2 * HEAD_DIM) % HEAD_DIM) // (HEAD_DIM // 2)
    n_seg = jnp.arange(HEAD_DIM) // (HEAD_DIM // 2)
    seg = (k_seg[:, None] == n_seg[None, :]).astype(BF16)
    return jnp.cos(ang), jnp.sin(ang), seg


def _attn_gains(g_q, g_k, cols):
    gains = jnp.concatenate([jnp.tile(g_q * (SCORE_SCALE * LOG2_E), Q_HEADS), jnp.tile(g_k, KV_HEADS)])
    return jnp.concatenate([_pair_layout_columns(gains),
                            jnp.ones((cols - TOK_WIDTH - KV_WIDTH,), F32)]).reshape(1, cols)


def kernel(x, mem, g_mix, g_ffn, w_in_a, g_v_a, w_spatial, b_spatial, w_in_b, g_q_b, g_k_b, g_mem, w_mem_kv,
           g_mq, g_mk, w_out, w_gate_up, w_down):
    batch, seq, _ = x.shape
    rows = batch * seq
    assert seq % IN_PROJ_ROWS == 0 and seq % ATTN_Q_ROWS == 0 and rows % FFN_ROWS == 0
    assert seq % MIX_ROWS_GMLP == 0 and seq % MIX_ROWS_ATTN == 0 and MIX_ROWS_GMLP % CHUNK == 0
    assert w_down.shape[1] % FFN_COLS == 0

    cos, sin, seg = _attn_tables(seq)
    mem_kt, mem_v = _mem_kv(mem, g_mem, w_mem_kv.astype(BF16), g_mk)
    w_in_a, w_spatial = w_in_a.astype(BF16), w_spatial.astype(BF16)
    w_in_b = _pair_layout_weights(w_in_b.astype(BF16))
    xs = x.reshape(rows, D_MODEL)
    for l in range(DEPTH):
        idx = l // 2
        out_casts = ((w_out, l),)
        ffn_casts = ((w_gate_up, l), (w_down, l))
        if l % 2 == 0:
            z, w_out_l = _in_proj(xs, g_mix[l], w_in_a, idx, casts=out_casts)
            xs, ffn_gate_up, ffn_down = _mix_out(xs, z, mem_kt, mem_v, g_mq[l], w_out_l, l, seq,
                                                 gmlp=(g_v_a[idx], w_spatial, b_spatial[idx], idx),
                                                 casts=ffn_casts)
        else:
            gain = _attn_gains(g_q_b[idx], g_k_b[idx], w_in_b.shape[2])
            z, w_out_l = _in_proj(xs, g_mix[l], w_in_b, idx, attn=(gain, cos, sin, seg), casts=out_casts)
            tok, ffn_gate_up, ffn_down = _attention(z, batch, seq, casts=ffn_casts)
            xs, = _mix_out(xs, z, mem_kt, mem_v, g_mq[l], w_out_l, l, seq, tok=tok)
        xs = _ffn(xs, g_ffn[l], ffn_gate_up, ffn_down)
    return xs.reshape(batch, seq, D_MODEL)
```

```python
import functools

import jax
import jax.numpy as jnp
from jax import lax
from jax.experimental import pallas as pl
from jax.experimental.pallas import tpu as pltpu

D_MODEL = 2048
DEPTH = 4
N_MEM = 256
GRID_W = 64
HEAD_DIM = 128
MEM_HEADS = 4
MEM_WIDTH = MEM_HEADS * HEAD_DIM
TOK_WIDTH = D_MODEL - MEM_WIDTH
CHUNK = 128
A_GROUPS = TOK_WIDTH // HEAD_DIM
Q_HEADS = TOK_WIDTH // HEAD_DIM
KV_HEADS = 4
Q_PER_KV = Q_HEADS // KV_HEADS
KV_WIDTH = KV_HEADS * HEAD_DIM
ROPE_THETA = 10000.0
ROPE_PAIRS = HEAD_DIM // 4
EPS = 1e-6
SCORE_SCALE = HEAD_DIM ** -0.5

VMEM_LIMIT_BYTES = 63 * 1024 * 1024
MXU_COLS = 256
LOG2_E = 1.4426950408889634

IN_PROJ_ROWS = 1024
IN_PROJ_DOT_COLS = 512
MIX_ROWS_GMLP = 512
MIX_ROWS_ATTN = 1024
ATTN_Q_ROWS = 1024
ATTN_CHAIN_ROWS = 256
FFN_ROWS = 1024
FFN_COLS = 512
NORM_CHUNK_ROWS = 256

BF16 = jnp.bfloat16
F32 = jnp.float32


def _rms_scale(x):
    return lax.rsqrt(jnp.mean(x * x, axis=-1, keepdims=True) + EPS)


def _gelu_exact(x):
    return 0.5 * x * (1.0 + lax.erf(x * (2.0 ** -0.5)))


def _dot(a, b):
    return jnp.dot(a, b, preferred_element_type=F32)


def _params(*semantics):
    return pltpu.CompilerParams(dimension_semantics=semantics, vmem_limit_bytes=VMEM_LIMIT_BYTES)


BF16_TILE_ROWS = 16
LANES = 128


def _weight_cast_specs(w, layer, n_steps, step_id):
    _, rows, cols = w.shape
    n_col = 1 if rows % (BF16_TILE_ROWS * n_steps) == 0 else 2
    n_row = n_steps // n_col
    assert n_row * n_col == n_steps and rows % (BF16_TILE_ROWS * n_row) == 0 and cols % (LANES * n_col) == 0
    block = (rows // n_row, cols // n_col)
    src = pl.BlockSpec((None,) + block, lambda *g: (layer, step_id(*g) // n_col, step_id(*g) % n_col))
    dst = pl.BlockSpec(block, lambda *g: (step_id(*g) // n_col, step_id(*g) % n_col))
    return src, dst, jax.ShapeDtypeStruct((rows, cols), BF16)


def _with_weight_casts(body, n_in, n_out, n_cast):
    def kernel(*refs):
        ins, refs = refs[:n_in], refs[n_in:]
        srcs, refs = refs[:n_cast], refs[n_cast:]
        outs, refs = refs[:n_out], refs[n_out:]
        dsts, scratch = refs[:n_cast], refs[n_cast:]
        for src, dst in zip(srcs, dsts):
            dst[...] = src[...].astype(BF16)
        body(*ins, *outs, *scratch)
    return kernel


def _mem_kv_kernel(mem_ref, g_mem_ref, w_ref, g_mk_ref, kt_ref, v_ref):
    m = mem_ref[...]
    h = (m * _rms_scale(m) * g_mem_ref[...]).astype(BF16)
    kv = _dot(h, w_ref[...])
    for hd in range(MEM_HEADS):
        k = kv[:, hd * HEAD_DIM:(hd + 1) * HEAD_DIM]
        k = k * _rms_scale(k) * g_mk_ref[...]
        kt_ref[hd * HEAD_DIM:(hd + 1) * HEAD_DIM, :] = k.T.astype(BF16)
    v_ref[...] = kv[:, MEM_WIDTH:].astype(BF16)


def _mem_kv(mem, g_mem, w_mem_kv, g_mk):
    batch = mem.shape[0]
    return pl.pallas_call(
        _mem_kv_kernel,
        grid=(DEPTH, batch),
        in_specs=[
            pl.BlockSpec((None, N_MEM, D_MODEL), lambda l, b: (b, 0, 0)),
            pl.BlockSpec((None, 1, D_MODEL), lambda l, b: (l, 0, 0)),
            pl.BlockSpec((None, D_MODEL, 2 * MEM_WIDTH), lambda l, b: (l, 0, 0)),
            pl.BlockSpec((None, 1, HEAD_DIM), lambda l, b: (l, 0, 0)),
        ],
        out_specs=[
            pl.BlockSpec((None, None, MEM_WIDTH, N_MEM), lambda l, b: (l, b, 0, 0)),
            pl.BlockSpec((None, None, N_MEM, MEM_WIDTH), lambda l, b: (l, b, 0, 0)),
        ],
        out_shape=[
            jax.ShapeDtypeStruct((DEPTH, batch, MEM_WIDTH, N_MEM), BF16),
            jax.ShapeDtypeStruct((DEPTH, batch, N_MEM, MEM_WIDTH), BF16),
        ],
        compiler_params=_params("arbitrary", "arbitrary"),
        name="mem_kv",
    )(mem, g_mem.reshape(DEPTH, 1, D_MODEL), w_mem_kv, g_mk.reshape(DEPTH, 1, HEAD_DIM))


def _norm_row_chunks(x_ref, g_ref, h_ref):
    for r in range(0, x_ref.shape[0], NORM_CHUNK_ROWS):
        rows = slice(r, r + NORM_CHUNK_ROWS)
        x = x_ref[rows, :]
        h = (x * _rms_scale(x) * g_ref[...]).astype(BF16)
        h_ref[rows, :] = h
        yield rows, h


def _project_column_chunks(x_ref, g_ref, h_ref, n_cols, project):
    for rows, h in _norm_row_chunks(x_ref, g_ref, h_ref):
        project(h, rows, 0)
    h = h_ref[...]
    for c0 in range(IN_PROJ_DOT_COLS, n_cols, IN_PROJ_DOT_COLS):
        project(h, slice(0, x_ref.shape[0]), c0)


def _in_proj_gmlp_kernel(x_ref, g_ref, w_ref, z_ref, h_ref, *, gelu_cols):
    def project(h, rows, c0):
        cols = slice(c0, c0 + IN_PROJ_DOT_COLS)
        acc = _dot(h, w_ref[:, cols])
        z_ref[rows, cols] = (_gelu_exact(acc) if c0 < gelu_cols else acc).astype(BF16)

    _project_column_chunks(x_ref, g_ref, h_ref, z_ref.shape[1], project)


def _pair_block_heads():
    blocks = [((2 * pair) * Q_PER_KV + g, (2 * pair + 1) * Q_PER_KV + g)
              for pair in range(KV_HEADS // 2) for g in range(Q_PER_KV)]
    return blocks + [(Q_HEADS + 2 * pair, Q_HEADS + 2 * pair + 1) for pair in range(KV_HEADS // 2)]


def _in_proj_attn_kernel(x_ref, g_ref, w_ref, gain_ref, cos_ref, sin_ref, seg_ref, perm_ref, z_ref, h_ref, wp_ref,
                         *, rope_cols):
    @pl.when(pl.program_id(0) == 0)
    def _():
        for blk, heads in enumerate(_pair_block_heads()):
            scattered = [_dot(w_ref[:, hd * HEAD_DIM:(hd + 1) * HEAD_DIM], perm_ref[m]) for m, hd in enumerate(heads)]
            wp_ref[:, blk * 2 * HEAD_DIM:(blk + 1) * 2 * HEAD_DIM] = (scattered[0] + scattered[1]).astype(BF16)

    def project(h, rows, c0):
        w_cols = wp_ref if c0 < rope_cols else w_ref
        acc = _dot(h, w_cols[:, c0:c0 + IN_PROJ_DOT_COLS])
        if c0 >= rope_cols:
            z_ref[rows, c0:c0 + IN_PROJ_DOT_COLS] = acc.astype(BF16)
            return
        cos, sin = cos_ref[rows, :], sin_ref[rows, :]
        for p0 in range(0, IN_PROJ_DOT_COLS, 2 * HEAD_DIM):
            t0, t1 = acc[:, p0:p0 + HEAD_DIM], acc[:, p0 + HEAD_DIM:p0 + 2 * HEAD_DIM]
            lanes0 = slice(c0 + p0, c0 + p0 + HEAD_DIM)
            lanes1 = slice(c0 + p0 + HEAD_DIM, c0 + p0 + 2 * HEAD_DIM)
            ss = t0 * t0 + t1 * t1
            hi = ss.astype(BF16)
            lo = (ss - hi.astype(F32)).astype(BF16)
            tot = _dot(jnp.concatenate([hi, lo], axis=1), seg_ref[...])
            r = lax.rsqrt(tot * (1.0 / HEAD_DIM) + EPS)
            n0 = t0 * r * gain_ref[:, lanes0]
            n1 = t1 * r * gain_ref[:, lanes1]
            z_ref[rows, lanes0] = (n0 * cos - n1 * sin).astype(BF16)
            z_ref[rows, lanes1] = (n1 * cos + n0 * sin).astype(BF16)

    _project_column_chunks(x_ref, g_ref, h_ref, z_ref.shape[1], project)


def _in_proj(x2d, g, w, layer, *, attn=None, casts=()):
    rows, _ = x2d.shape
    cols = w.shape[2]
    bm = IN_PROJ_ROWS
    assert cols % IN_PROJ_DOT_COLS == 0
    resident = pl.Buffered(1)
    in_specs = [
        pl.BlockSpec((bm, D_MODEL), lambda i: (i, 0)),
        pl.BlockSpec((1, D_MODEL), lambda i: (0, 0)),
        pl.BlockSpec((None, D_MODEL, cols), lambda i: (layer, 0, 0), pipeline_mode=resident),
    ]
    args = [x2d, g.reshape(1, D_MODEL), w]
    scratch_shapes = [pltpu.VMEM((bm, D_MODEL), BF16)]
    if attn is None:
        assert (2 * TOK_WIDTH) % IN_PROJ_DOT_COLS == 0
        body = functools.partial(_in_proj_gmlp_kernel, gelu_cols=2 * TOK_WIDTH)
    else:
        gain, cos, sin, seg, perm = attn
        seq_tiles = cos.shape[0] // bm
        rope_cols = TOK_WIDTH + KV_WIDTH
        assert rope_cols % IN_PROJ_DOT_COLS == 0
        body = functools.partial(_in_proj_attn_kernel, rope_cols=rope_cols)
        in_specs += [
            pl.BlockSpec((1, cols), lambda i: (0, 0)),
            pl.BlockSpec((bm, HEAD_DIM), lambda i: (i % seq_tiles, 0)),
            pl.BlockSpec((bm, HEAD_DIM), lambda i: (i % seq_tiles, 0)),
            pl.BlockSpec((2 * HEAD_DIM, HEAD_DIM), lambda i: (0, 0)),
            pl.BlockSpec((2, HEAD_DIM, 2 * HEAD_DIM), lambda i: (0, 0, 0)),
        ]
        args += [gain, cos, sin, seg, perm]
        scratch_shapes.append(pltpu.VMEM((D_MODEL, rope_cols), BF16))
    cast_specs = [_weight_cast_specs(*cast, rows // bm, lambda i: i) for cast in casts]
    return pl.pallas_call(
        _with_weight_casts(body, len(in_specs), 1, len(casts)),
        grid=(rows // bm,),
        in_specs=in_specs + [src for src, _, _ in cast_specs],
        out_specs=[pl.BlockSpec((bm, cols), lambda i: (i, 0))] + [dst for _, dst, _ in cast_specs],
        out_shape=[jax.ShapeDtypeStruct((rows, cols), BF16)] + [shape for _, _, shape in cast_specs],
        scratch_shapes=scratch_shapes,
        compiler_params=_params("arbitrary"),
        name="in_proj_gmlp" if attn is None else "in_proj_attn",
    )(*args, *[cast[0] for cast in casts])


_NT_DIMS = (((1,), (1,)), ((), ()))


def _attention_kernel(q_ref, k_ref, v_ref, o_ref, v1_ref):
    @pl.when(pl.program_id(2) == 0)
    def _():
        ones_col = lax.broadcasted_iota(jnp.int32, (v_ref.shape[0], HEAD_DIM), 1) == 0
        v1_ref[:, :HEAD_DIM] = v_ref[...]
        v1_ref[:, HEAD_DIM:] = jnp.where(ones_col, 1.0, 0.0).astype(BF16)

    k = k_ref[...]
    v = v1_ref[...]
    half = HEAD_DIM // 2
    lane = lax.broadcasted_iota(jnp.int32, (ATTN_CHAIN_ROWS, 2 * HEAD_DIM), 1) % HEAD_DIM
    first_lane = (pl.program_id(1) % 2) * half
    keep = jnp.logical_and(lane >= first_lane, lane < first_lane + half)
    for r in range(0, q_ref.shape[0], ATTN_CHAIN_ROWS):
        rows = slice(r, r + ATTN_CHAIN_ROWS)
        for g in range(Q_PER_KV):
            q = q_ref[rows, g * 2 * HEAD_DIM:(g + 1) * 2 * HEAD_DIM]
            q = jnp.where(keep, q, jnp.zeros_like(q))
            s = lax.dot_general(q, k, _NT_DIMS, preferred_element_type=F32)
            p = jnp.exp2((s - jnp.max(s, axis=-1, keepdims=True)).astype(BF16))
            o = _dot(p, v)
            o = o[:, :HEAD_DIM] / o[:, HEAD_DIM:HEAD_DIM + 1]
            o_ref[rows, g * HEAD_DIM:(g + 1) * HEAD_DIM] = o.astype(BF16)


def _attention(z, batch, seq, casts=()):
    bq = ATTN_Q_ROWS
    q_tiles = seq // bq
    pair_width = 2 * HEAD_DIM
    k_block0 = TOK_WIDTH // pair_width
    v_block0 = (TOK_WIDTH + KV_WIDTH) // HEAD_DIM
    grid = (batch, KV_HEADS, q_tiles)
    cast_specs = [_weight_cast_specs(*cast, batch * KV_HEADS * q_tiles,
                                     lambda b, h, i: (b * KV_HEADS + h) * q_tiles + i) for cast in casts]
    in_specs = [
        pl.BlockSpec((bq, Q_PER_KV * pair_width), lambda b, h, i: (b * q_tiles + i, h // 2)),
        pl.BlockSpec((seq, pair_width), lambda b, h, i: (b, k_block0 + h // 2)),
        pl.BlockSpec((seq, HEAD_DIM), lambda b, h, i: (b, v_block0 + h)),
    ]
    return pl.pallas_call(
        _with_weight_casts(_attention_kernel, len(in_specs), 1, len(casts)),
        grid=grid,
        in_specs=in_specs + [src for src, _, _ in cast_specs],
        out_specs=[pl.BlockSpec((bq, Q_PER_KV * HEAD_DIM), lambda b, h, i: (b * q_tiles + i, h))]
        + [dst for _, dst, _ in cast_specs],
        out_shape=[jax.ShapeDtypeStruct((batch * seq, TOK_WIDTH), BF16)] + [shape for _, _, shape in cast_specs],
        scratch_shapes=[pltpu.VMEM((seq, 2 * HEAD_DIM), BF16)],
        compiler_params=_params("parallel", "parallel", "arbitrary"),
        name="attention",
    )(z, z, z, *[cast[0] for cast in casts])


def _memory_attention_into(mixed_ref, qm_ref, g_mq_ref, kt_ref, v_ref):
    for hd in range(MEM_HEADS):
        lanes = slice(hd * HEAD_DIM, (hd + 1) * HEAD_DIM)
        q = qm_ref[:, lanes].astype(F32)
        q = (q * _rms_scale(q) * g_mq_ref[...] * SCORE_SCALE).astype(BF16)
        s = _dot(q, kt_ref[lanes, :])
        p = jnp.exp(s - jnp.max(s, axis=-1, keepdims=True))
        denom = jnp.sum(p, axis=-1, keepdims=True)
        o = _dot(p.astype(BF16), v_ref[:, lanes]) / denom
        mixed_ref[:, TOK_WIDTH + hd * HEAD_DIM:TOK_WIDTH + (hd + 1) * HEAD_DIM] = o.astype(BF16)


def _mix_out_gmlp_kernel(u_ref, v_ref, qm_ref, g_v_ref, ws_ref, bs_ref, g_mq_ref, kt_ref, mv_ref,
                         w_out_ref, x_ref, o_ref, mixed_ref):
    v = v_ref[...].astype(F32)
    vn = (v * _rms_scale(v) * g_v_ref[...]).astype(BF16)
    for c in range(v.shape[0] // CHUNK):
        rows = slice(c * CHUNK, (c + 1) * CHUNK)
        for g in range(A_GROUPS):
            lanes = slice(g * HEAD_DIM, (g + 1) * HEAD_DIM)
            s = _dot(ws_ref[g], vn[rows, lanes]) + bs_ref[g]
            mixed_ref[rows, lanes] = (u_ref[rows, lanes].astype(F32) * s).astype(BF16)
    _memory_attention_into(mixed_ref, qm_ref, g_mq_ref, kt_ref, mv_ref)
    o_ref[...] = x_ref[...] + _dot(mixed_ref[...], w_out_ref[...])


def _mix_out_attn_kernel(t_ref, qm_ref, g_mq_ref, kt_ref, mv_ref, w_out_ref, x_ref, o_ref, mixed_ref):
    mixed_ref[:, :TOK_WIDTH] = t_ref[...]
    _memory_attention_into(mixed_ref, qm_ref, g_mq_ref, kt_ref, mv_ref)
    o_ref[...] = x_ref[...] + _dot(mixed_ref[...], w_out_ref[...])


def _mix_out(x2d, z, mem_kt, mem_v, g_mq, w_out, layer, seq, *, gmlp=None, tok=None, casts=()):
    rows = x2d.shape[0]
    bm = MIX_ROWS_GMLP if gmlp is not None else MIX_ROWS_ATTN
    seq_tiles = seq // bm
    qm_col = (z.shape[1] - MEM_WIDTH) // MEM_WIDTH
    shared_specs = [
        pl.BlockSpec((bm, MEM_WIDTH), lambda i: (i, qm_col)),
    ]
    tail_specs = [
        pl.BlockSpec((1, HEAD_DIM), lambda i: (0, 0)),
        pl.BlockSpec((None, None, MEM_WIDTH, N_MEM), lambda i: (layer, i // seq_tiles, 0, 0)),
        pl.BlockSpec((None, None, N_MEM, MEM_WIDTH), lambda i: (layer, i // seq_tiles, 0, 0)),
        pl.BlockSpec((D_MODEL, D_MODEL), lambda i: (0, 0), pipeline_mode=pl.Buffered(1)),
        pl.BlockSpec((bm, D_MODEL), lambda i: (i, 0)),
    ]
    tail_args = [g_mq.reshape(1, HEAD_DIM), mem_kt, mem_v, w_out, x2d]
    if gmlp is not None:
        g_v, w_s, b_s, gmlp_layer = gmlp
        body = _mix_out_gmlp_kernel
        in_specs = [
            pl.BlockSpec((bm, TOK_WIDTH), lambda i: (i, 0)),
            pl.BlockSpec((bm, TOK_WIDTH), lambda i: (i, 1)),
        ] + shared_specs + [
            pl.BlockSpec((1, TOK_WIDTH), lambda i: (0, 0)),
            pl.BlockSpec((None, A_GROUPS, CHUNK, CHUNK), lambda i: (gmlp_layer, 0, 0, 0)),
            pl.BlockSpec((A_GROUPS, CHUNK, 1), lambda i: (0, 0, 0)),
        ] + tail_specs
        args = [z, z, z, g_v.reshape(1, TOK_WIDTH), w_s, b_s.reshape(A_GROUPS, CHUNK, 1)] + tail_args
        name = "mix_out_gmlp"
    else:
        body = _mix_out_attn_kernel
        in_specs = [pl.BlockSpec((bm, TOK_WIDTH), lambda i: (i, 0))] + shared_specs + tail_specs
        args = [tok, z] + tail_args
        name = "mix_out_attn"
    cast_specs = [_weight_cast_specs(*cast, rows // bm, lambda i: i) for cast in casts]
    return pl.pallas_call(
        _with_weight_casts(body, len(in_specs), 1, len(casts)),
        grid=(rows // bm,),
        in_specs=in_specs + [src for src, _, _ in cast_specs],
        out_specs=[pl.BlockSpec((bm, D_MODEL), lambda i: (i, 0))] + [dst for _, dst, _ in cast_specs],
        out_shape=[jax.ShapeDtypeStruct((rows, D_MODEL), F32)] + [shape for _, _, shape in cast_specs],
        scratch_shapes=[pltpu.VMEM((bm, D_MODEL), BF16)],
        compiler_params=_params("parallel"),
        name=name,
    )(*args, *[cast[0] for cast in casts])


def _ffn_kernel(x_ref, g_ref, wg_ref, wu_ref, wd_ref, o_ref, h_ref):
    j = pl.program_id(1)

    def contribution(h):
        gate = _dot(h, wg_ref[...])
        up = _dot(h, wu_ref[...])
        act = (jax.nn.silu(gate) * up).astype(BF16)
        return _dot(act, wd_ref[...])

    @pl.when(j == 0)
    def _():
        for rows, h in _norm_row_chunks(x_ref, g_ref, h_ref):
            o_ref[rows, :] = x_ref[rows, :] + contribution(h)

    @pl.when(j > 0)
    def _():
        o_ref[...] += contribution(h_ref[...])


def _ffn(x2d, g, w_gate_up, w_down):
    rows = x2d.shape[0]
    d_ff = w_down.shape[0]
    bm, bf = FFN_ROWS, FFN_COLS
    ff_tiles = d_ff // bf
    return pl.pallas_call(
        _ffn_kernel,
        grid=(rows // bm, ff_tiles),
        in_specs=[
            pl.BlockSpec((bm, D_MODEL), lambda i, j: (i, 0)),
            pl.BlockSpec((1, D_MODEL), lambda i, j: (0, 0)),
            pl.BlockSpec((D_MODEL, bf), lambda i, j: (0, j)),
            pl.BlockSpec((D_MODEL, bf), lambda i, j: (0, ff_tiles + j)),
            pl.BlockSpec((bf, D_MODEL), lambda i, j: (j, 0)),
        ],
        out_specs=pl.BlockSpec((bm, D_MODEL), lambda i, j: (i, 0)),
        out_shape=jax.ShapeDtypeStruct((rows, D_MODEL), F32),
        scratch_shapes=[pltpu.VMEM((bm, D_MODEL), BF16)],
        compiler_params=_params("parallel", "arbitrary"),
        name="ffn",
    )(x2d, g.reshape(1, D_MODEL), w_gate_up, w_gate_up, w_down)


def _pair_layout_columns(a):
    lead = a.shape[:-1]
    n = len(lead)
    axes = tuple(range(n))
    half_pairs = KV_HEADS // 2
    q = a[..., :TOK_WIDTH].reshape(lead + (half_pairs, 2, Q_PER_KV, HEAD_DIM))
    q = q.transpose(axes + (n, n + 2, n + 1, n + 3)).reshape(lead + (half_pairs * Q_PER_KV, 2, HEAD_DIM))
    k = a[..., TOK_WIDTH:].reshape(lead + (half_pairs, 2, HEAD_DIM))
    blocks = jnp.concatenate([q, k], axis=n)
    blocks = blocks.reshape(lead + (blocks.shape[n], 2, 2, 2, ROPE_PAIRS))
    blocks = blocks.transpose(axes + (n, n + 3, n + 1, n + 2, n + 4))
    return blocks.reshape(lead + (TOK_WIDTH + KV_WIDTH,))


def _attn_tables(seq):
    n_rows = seq // GRID_W
    rows = jnp.broadcast_to(jnp.arange(n_rows)[:, None], (n_rows, GRID_W)).reshape(seq)
    cols = jnp.broadcast_to(jnp.arange(GRID_W)[None, :], (n_rows, GRID_W)).reshape(seq)
    freqs = ROPE_THETA ** (-jnp.arange(ROPE_PAIRS, dtype=F32) / ROPE_PAIRS)
    ang_r = rows.astype(F32)[:, None] * freqs
    ang_c = cols.astype(F32)[:, None] * freqs
    ang = jnp.concatenate([ang_r, ang_c, ang_r, ang_c], axis=-1)
    k_seg = (jnp.arange(2 * HEAD_DIM) % HEAD_DIM) // (HEAD_DIM // 2)
    n_seg = jnp.arange(HEAD_DIM) // (HEAD_DIM // 2)
    seg = (k_seg[:, None] == n_seg[None, :]).astype(BF16)
    d = jnp.arange(HEAD_DIM)
    axis, half, pair_index = d // (2 * ROPE_PAIRS), (d // ROPE_PAIRS) % 2, d % ROPE_PAIRS
    target = half[None, :] * HEAD_DIM + jnp.arange(2)[:, None] * (HEAD_DIM // 2) + axis[None, :] * ROPE_PAIRS \
        + pair_index[None, :]
    perm = (target[:, :, None] == jnp.arange(2 * HEAD_DIM)[None, None, :]).astype(BF16)
    return jnp.cos(ang), jnp.sin(ang), seg, perm


def _attn_gains(g_q, g_k, cols):
    gains = jnp.concatenate([jnp.tile(g_q * (SCORE_SCALE * LOG2_E), Q_HEADS), jnp.tile(g_k, KV_HEADS)])
    return jnp.concatenate([_pair_layout_columns(gains),
                            jnp.ones((cols - TOK_WIDTH - KV_WIDTH,), F32)]).reshape(1, cols)


def kernel(x, mem, g_mix, g_ffn, w_in_a, g_v_a, w_spatial, b_spatial, w_in_b, g_q_b, g_k_b, g_mem, w_mem_kv,
           g_mq, g_mk, w_out, w_gate_up, w_down):
    batch, seq, _ = x.shape
    rows = batch * seq
    assert seq % IN_PROJ_ROWS == 0 and seq % ATTN_Q_ROWS == 0 and rows % FFN_ROWS == 0
    assert seq % MIX_ROWS_GMLP == 0 and seq % MIX_ROWS_ATTN == 0 and MIX_ROWS_GMLP % CHUNK == 0
    assert w_down.shape[1] % FFN_COLS == 0

    cos, sin, seg, perm = _attn_tables(seq)
    mem_kt, mem_v = _mem_kv(mem, g_mem, w_mem_kv.astype(BF16), g_mk)
    w_in_a, w_spatial = w_in_a.astype(BF16), w_spatial.astype(BF16)
    w_in_b = w_in_b.astype(BF16)
    xs = x.reshape(rows, D_MODEL)
    for l in range(DEPTH):
        idx = l // 2
        ffn_casts = ((w_gate_up, l), (w_down, l))
        if l % 2 == 0:
            z, w_out_l, w_out_next = _in_proj(xs, g_mix[l], w_in_a, idx, casts=((w_out, l), (w_out, l + 1)))
            xs, ffn_gate_up, ffn_down = _mix_out(xs, z, mem_kt, mem_v, g_mq[l], w_out_l, l, seq,
                                                 gmlp=(g_v_a[idx], w_spatial, b_spatial[idx], idx),
                                                 casts=ffn_casts)
        else:
            gain = _attn_gains(g_q_b[idx], g_k_b[idx], w_in_b.shape[2])
            z, = _in_proj(xs, g_mix[l], w_in_b, idx, attn=(gain, cos, sin, seg, perm))
            tok, ffn_gate_up, ffn_down = _attention(z, batch, seq, casts=ffn_casts)
            xs, = _mix_out(xs, z, mem_kt, mem_v, g_mq[l], w_out_next, l, seq, tok=tok)
        xs = _ffn(xs, g_ffn[l], ffn_gate_up, ffn_down)
    return xs.reshape(batch, seq, D_MODEL)
```

```python
import functools

import jax
import jax.numpy as jnp
from jax import lax
from jax.experimental import pallas as pl
from jax.experimental.pallas import tpu as pltpu

D_MODEL = 2048
DEPTH = 4
N_MEM = 256
GRID_W = 64
HEAD_DIM = 128
MEM_HEADS = 4
MEM_WIDTH = MEM_HEADS * HEAD_DIM
TOK_WIDTH = D_MODEL - MEM_WIDTH
CHUNK = 128
A_GROUPS = TOK_WIDTH // HEAD_DIM
Q_HEADS = TOK_WIDTH // HEAD_DIM
KV_HEADS = 4
Q_PER_KV = Q_HEADS // KV_HEADS
KV_WIDTH = KV_HEADS * HEAD_DIM
ROPE_THETA = 10000.0
ROPE_PAIRS = HEAD_DIM // 4
EPS = 1e-6
SCORE_SCALE = HEAD_DIM ** -0.5

VMEM_LIMIT_BYTES = 63 * 1024 * 1024
MXU_COLS = 256
LOG2_E = 1.4426950408889634

IN_PROJ_ROWS = 1024
IN_PROJ_DOT_COLS = 512
MIX_ROWS_GMLP = 512
MIX_ROWS_ATTN = 1024
ATTN_Q_ROWS = 1024
ATTN_CHAIN_ROWS = 256
FFN_ROWS = 1024
FFN_COLS = 512
NORM_CHUNK_ROWS = 256

BF16 = jnp.bfloat16
F32 = jnp.float32


def _rms_scale(x):
    return lax.rsqrt(jnp.mean(x * x, axis=-1, keepdims=True) + EPS)


def _gelu_exact(x):
    return 0.5 * x * (1.0 + lax.erf(x * (2.0 ** -0.5)))


def _dot(a, b):
    return jnp.dot(a, b, preferred_element_type=F32)


def _params(*semantics):
    return pltpu.CompilerParams(dimension_semantics=semantics, vmem_limit_bytes=VMEM_LIMIT_BYTES)


BF16_TILE_ROWS = 16
LANES = 128


def _weight_cast_specs(w, layer, n_steps, step_id):
    _, rows, cols = w.shape
    n_col = 1 if rows % (BF16_TILE_ROWS * n_steps) == 0 else 2
    n_row = n_steps // n_col
    assert n_row * n_col == n_steps and rows % (BF16_TILE_ROWS * n_row) == 0 and cols % (LANES * n_col) == 0
    block = (rows // n_row, cols // n_col)
    src = pl.BlockSpec((None,) + block, lambda *g: (layer, step_id(*g) // n_col, step_id(*g) % n_col))
    dst = pl.BlockSpec(block, lambda *g: (step_id(*g) // n_col, step_id(*g) % n_col))
    return src, dst, jax.ShapeDtypeStruct((rows, cols), BF16)


def _with_weight_casts(body, n_in, n_out, n_cast):
    def kernel(*refs):
        ins, refs = refs[:n_in], refs[n_in:]
        srcs, refs = refs[:n_cast], refs[n_cast:]
        outs, refs = refs[:n_out], refs[n_out:]
        dsts, scratch = refs[:n_cast], refs[n_cast:]
        for src, dst in zip(srcs, dsts):
            dst[...] = src[...].astype(BF16)
        body(*ins, *outs, *scratch)
    return kernel


def _mem_kv_kernel(mem_ref, g_mem_ref, w_ref, g_mk_ref, kt_ref, v_ref):
    m = mem_ref[...]
    h = (m * _rms_scale(m) * g_mem_ref[...]).astype(BF16)
    kv = _dot(h, w_ref[...].astype(BF16))
    for hd in range(MEM_HEADS):
        k = kv[:, hd * HEAD_DIM:(hd + 1) * HEAD_DIM]
        k = k * _rms_scale(k) * g_mk_ref[...]
        kt_ref[hd * HEAD_DIM:(hd + 1) * HEAD_DIM, :] = k.T.astype(BF16)
    v_ref[...] = kv[:, MEM_WIDTH:].astype(BF16)


def _mem_kv(mem, g_mem, w_mem_kv, g_mk):
    batch = mem.shape[0]
    return pl.pallas_call(
        _mem_kv_kernel,
        grid=(DEPTH, batch),
        in_specs=[
            pl.BlockSpec((None, N_MEM, D_MODEL), lambda l, b: (b, 0, 0)),
            pl.BlockSpec((None, 1, D_MODEL), lambda l, b: (l, 0, 0)),
            pl.BlockSpec((None, D_MODEL, 2 * MEM_WIDTH), lambda l, b: (l, 0, 0)),
            pl.BlockSpec((None, 1, HEAD_DIM), lambda l, b: (l, 0, 0)),
        ],
        out_specs=[
            pl.BlockSpec((None, None, MEM_WIDTH, N_MEM), lambda l, b: (l, b, 0, 0)),
            pl.BlockSpec((None, None, N_MEM, MEM_WIDTH), lambda l, b: (l, b, 0, 0)),
        ],
        out_shape=[
            jax.ShapeDtypeStruct((DEPTH, batch, MEM_WIDTH, N_MEM), BF16),
            jax.ShapeDtypeStruct((DEPTH, batch, N_MEM, MEM_WIDTH), BF16),
        ],
        compiler_params=_params("arbitrary", "arbitrary"),
        name="mem_kv",
    )(mem, g_mem.reshape(DEPTH, 1, D_MODEL), w_mem_kv, g_mk.reshape(DEPTH, 1, HEAD_DIM))


def _norm_row_chunks(x_ref, g_ref, h_ref):
    for r in range(0, x_ref.shape[0], NORM_CHUNK_ROWS):
        rows = slice(r, r + NORM_CHUNK_ROWS)
        x = x_ref[rows, :]
        h = (x * _rms_scale(x) * g_ref[...]).astype(BF16)
        h_ref[rows, :] = h
        yield rows, h


def _project_column_chunks(x_ref, g_ref, h_ref, n_cols, project):
    for rows, h in _norm_row_chunks(x_ref, g_ref, h_ref):
        project(h, rows, 0)
    h = h_ref[...]
    for c0 in range(IN_PROJ_DOT_COLS, n_cols, IN_PROJ_DOT_COLS):
        project(h, slice(0, x_ref.shape[0]), c0)


def _in_proj_gmlp_kernel(x_ref, g_ref, w_ref, z_ref, h_ref, *, gelu_cols):
    def project(h, rows, c0):
        cols = slice(c0, c0 + IN_PROJ_DOT_COLS)
        acc = _dot(h, w_ref[:, cols])
        z_ref[rows, cols] = (_gelu_exact(acc) if c0 < gelu_cols else acc).astype(BF16)

    _project_column_chunks(x_ref, g_ref, h_ref, z_ref.shape[1], project)


def _pair_block_heads():
    blocks = [((2 * pair) * Q_PER_KV + g, (2 * pair + 1) * Q_PER_KV + g)
              for pair in range(KV_HEADS // 2) for g in range(Q_PER_KV)]
    return blocks + [(Q_HEADS + 2 * pair, Q_HEADS + 2 * pair + 1) for pair in range(KV_HEADS // 2)]


def _in_proj_attn_kernel(x_ref, g_ref, w_ref, gain_ref, cos_ref, sin_ref, seg_ref, perm_ref, z_ref, h_ref, wp_ref,
                         *, rope_cols):
    @pl.when(pl.program_id(0) == 0)
    def _():
        for blk, heads in enumerate(_pair_block_heads()):
            scattered = [_dot(w_ref[:, hd * HEAD_DIM:(hd + 1) * HEAD_DIM], perm_ref[m]) for m, hd in enumerate(heads)]
            wp_ref[:, blk * 2 * HEAD_DIM:(blk + 1) * 2 * HEAD_DIM] = (scattered[0] + scattered[1]).astype(BF16)

    def project(h, rows, c0):
        w_cols = wp_ref if c0 < rope_cols else w_ref
        acc = _dot(h, w_cols[:, c0:c0 + IN_PROJ_DOT_COLS])
        if c0 >= rope_cols:
            z_ref[rows, c0:c0 + IN_PROJ_DOT_COLS] = acc.astype(BF16)
            return
        cos, sin = cos_ref[rows, :], sin_ref[rows, :]
        for p0 in range(0, IN_PROJ_DOT_COLS, 2 * HEAD_DIM):
            t0, t1 = acc[:, p0:p0 + HEAD_DIM], acc[:, p0 + HEAD_DIM:p0 + 2 * HEAD_DIM]
            lanes0 = slice(c0 + p0, c0 + p0 + HEAD_DIM)
            lanes1 = slice(c0 + p0 + HEAD_DIM, c0 + p0 + 2 * HEAD_DIM)
            ss = t0 * t0 + t1 * t1
            hi = ss.astype(BF16)
            lo = (ss - hi.astype(F32)).astype(BF16)
            tot = _dot(jnp.concatenate([hi, lo], axis=1), seg_ref[...])
            r = lax.rsqrt(tot * (1.0 / HEAD_DIM) + EPS)
            n0 = t0 * r * gain_ref[:, lanes0]
            n1 = t1 * r * gain_ref[:, lanes1]
            z_ref[rows, lanes0] = (n0 * cos - n1 * sin).astype(BF16)
            z_ref[rows, lanes1] = (n1 * cos + n0 * sin).astype(BF16)

    _project_column_chunks(x_ref, g_ref, h_ref, z_ref.shape[1], project)


def _in_proj(x2d, g, w, layer, *, attn=None, casts=()):
    rows, _ = x2d.shape
    cols = w.shape[-1]
    bm = IN_PROJ_ROWS
    assert cols % IN_PROJ_DOT_COLS == 0
    resident = pl.Buffered(1)
    if layer is None:
        w_spec = pl.BlockSpec((D_MODEL, cols), lambda i: (0, 0), pipeline_mode=resident)
    else:
        w_spec = pl.BlockSpec((None, D_MODEL, cols), lambda i: (layer, 0, 0), pipeline_mode=resident)
    in_specs = [
        pl.BlockSpec((bm, D_MODEL), lambda i: (i, 0)),
        pl.BlockSpec((1, D_MODEL), lambda i: (0, 0)),
        w_spec,
    ]
    args = [x2d, g.reshape(1, D_MODEL), w]
    scratch_shapes = [pltpu.VMEM((bm, D_MODEL), BF16)]
    if attn is None:
        assert (2 * TOK_WIDTH) % IN_PROJ_DOT_COLS == 0
        body = functools.partial(_in_proj_gmlp_kernel, gelu_cols=2 * TOK_WIDTH)
    else:
        gain, cos, sin, seg, perm = attn
        seq_tiles = cos.shape[0] // bm
        rope_cols = TOK_WIDTH + KV_WIDTH
        assert rope_cols % IN_PROJ_DOT_COLS == 0
        body = functools.partial(_in_proj_attn_kernel, rope_cols=rope_cols)
        in_specs += [
            pl.BlockSpec((1, cols), lambda i: (0, 0)),
            pl.BlockSpec((bm, HEAD_DIM), lambda i: (i % seq_tiles, 0)),
            pl.BlockSpec((bm, HEAD_DIM), lambda i: (i % seq_tiles, 0)),
            pl.BlockSpec((2 * HEAD_DIM, HEAD_DIM), lambda i: (0, 0)),
            pl.BlockSpec((2, HEAD_DIM, 2 * HEAD_DIM), lambda i: (0, 0, 0)),
        ]
        args += [gain, cos, sin, seg, perm]
        scratch_shapes.append(pltpu.VMEM((D_MODEL, rope_cols), BF16))
    cast_specs = [_weight_cast_specs(*cast, rows // bm, lambda i: i) for cast in casts]
    return pl.pallas_call(
        _with_weight_casts(body, len(in_specs), 1, len(casts)),
        grid=(rows // bm,),
        in_specs=in_specs + [src for src, _, _ in cast_specs],
        out_specs=[pl.BlockSpec((bm, cols), lambda i: (i, 0))] + [dst for _, dst, _ in cast_specs],
        out_shape=[jax.ShapeDtypeStruct((rows, cols), BF16)] + [shape for _, _, shape in cast_specs],
        scratch_shapes=scratch_shapes,
        compiler_params=_params("arbitrary"),
        name="in_proj_gmlp" if attn is None else "in_proj_attn",
    )(*args, *[cast[0] for cast in casts])


_NT_DIMS = (((1,), (1,)), ((), ()))


def _attention_kernel(q_ref, k_ref, v_ref, o_ref, v1_ref):
    @pl.when(pl.program_id(2) == 0)
    def _():
        ones_col = lax.broadcasted_iota(jnp.int32, (v_ref.shape[0], HEAD_DIM), 1) == 0
        v1_ref[:, :HEAD_DIM] = v_ref[...]
        v1_ref[:, HEAD_DIM:] = jnp.where(ones_col, 1.0, 0.0).astype(BF16)

    k = k_ref[...]
    v = v1_ref[...]
    half = HEAD_DIM // 2
    lane = lax.broadcasted_iota(jnp.int32, (ATTN_CHAIN_ROWS, 2 * HEAD_DIM), 1) % HEAD_DIM
    first_lane = (pl.program_id(1) % 2) * half
    keep = jnp.logical_and(lane >= first_lane, lane < first_lane + half)
    for r in range(0, q_ref.shape[0], ATTN_CHAIN_ROWS):
        rows = slice(r, r + ATTN_CHAIN_ROWS)
        for g in range(Q_PER_KV):
            q = q_ref[rows, g * 2 * HEAD_DIM:(g + 1) * 2 * HEAD_DIM]
            q = jnp.where(keep, q, jnp.zeros_like(q))
            s = lax.dot_general(q, k, _NT_DIMS, preferred_element_type=F32)
            p = jnp.exp2((s - jnp.max(s, axis=-1, keepdims=True)).astype(BF16))
            o = _dot(p, v)
            o = o[:, :HEAD_DIM] / o[:, HEAD_DIM:HEAD_DIM + 1]
            o_ref[rows, g * HEAD_DIM:(g + 1) * HEAD_DIM] = o.astype(BF16)


def _attention(z, batch, seq, casts=()):
    bq = ATTN_Q_ROWS
    q_tiles = seq // bq
    pair_width = 2 * HEAD_DIM
    k_block0 = TOK_WIDTH // pair_width
    v_block0 = (TOK_WIDTH + KV_WIDTH) // HEAD_DIM
    grid = (batch, KV_HEADS, q_tiles)
    cast_specs = [_weight_cast_specs(*cast, batch * KV_HEADS * q_tiles,
                                     lambda b, h, i: (b * KV_HEADS + h) * q_tiles + i) for cast in casts]
    in_specs = [
        pl.BlockSpec((bq, Q_PER_KV * pair_width), lambda b, h, i: (b * q_tiles + i, h // 2)),
        pl.BlockSpec((seq, pair_width), lambda b, h, i: (b, k_block0 + h // 2)),
        pl.BlockSpec((seq, HEAD_DIM), lambda b, h, i: (b, v_block0 + h)),
    ]
    return pl.pallas_call(
        _with_weight_casts(_attention_kernel, len(in_specs), 1, len(casts)),
        grid=grid,
        in_specs=in_specs + [src for src, _, _ in cast_specs],
        out_specs=[pl.BlockSpec((bq, Q_PER_KV * HEAD_DIM), lambda b, h, i: (b * q_tiles + i, h))]
        + [dst for _, dst, _ in cast_specs],
        out_shape=[jax.ShapeDtypeStruct((batch * seq, TOK_WIDTH), BF16)] + [shape for _, _, shape in cast_specs],
        scratch_shapes=[pltpu.VMEM((seq, 2 * HEAD_DIM), BF16)],
        compiler_params=_params("parallel", "parallel", "arbitrary"),
        name="attention",
    )(z, z, z, *[cast[0] for cast in casts])


def _memory_attention_into(mixed_ref, qm_ref, g_mq_ref, kt_ref, v_ref):
    for hd in range(MEM_HEADS):
        lanes = slice(hd * HEAD_DIM, (hd + 1) * HEAD_DIM)
        q = qm_ref[:, lanes].astype(F32)
        q = (q * _rms_scale(q) * g_mq_ref[...] * SCORE_SCALE).astype(BF16)
        s = _dot(q, kt_ref[lanes, :])
        p = jnp.exp(s - jnp.max(s, axis=-1, keepdims=True))
        denom = jnp.sum(p, axis=-1, keepdims=True)
        o = _dot(p.astype(BF16), v_ref[:, lanes]) / denom
        mixed_ref[:, TOK_WIDTH + hd * HEAD_DIM:TOK_WIDTH + (hd + 1) * HEAD_DIM] = o.astype(BF16)


def _mix_out_gmlp_kernel(u_ref, v_ref, qm_ref, g_v_ref, ws_ref, bs_ref, g_mq_ref, kt_ref, mv_ref,
                         w_out_ref, x_ref, o_ref, mixed_ref):
    v = v_ref[...].astype(F32)
    vn = (v * _rms_scale(v) * g_v_ref[...]).astype(BF16)
    for c in range(v.shape[0] // CHUNK):
        rows = slice(c * CHUNK, (c + 1) * CHUNK)
        for g in range(A_GROUPS):
            lanes = slice(g * HEAD_DIM, (g + 1) * HEAD_DIM)
            s = _dot(ws_ref[g], vn[rows, lanes]) + bs_ref[g]
            mixed_ref[rows, lanes] = (u_ref[rows, lanes].astype(F32) * s).astype(BF16)
    _memory_attention_into(mixed_ref, qm_ref, g_mq_ref, kt_ref, mv_ref)
    o_ref[...] = x_ref[...] + _dot(mixed_ref[...], w_out_ref[...])


def _mix_out_attn_kernel(t_ref, qm_ref, g_mq_ref, kt_ref, mv_ref, w_out_ref, x_ref, o_ref, mixed_ref):
    mixed_ref[:, :TOK_WIDTH] = t_ref[...]
    _memory_attention_into(mixed_ref, qm_ref, g_mq_ref, kt_ref, mv_ref)
    o_ref[...] = x_ref[...] + _dot(mixed_ref[...], w_out_ref[...])


def _mix_out(x2d, z, mem_kt, mem_v, g_mq, w_out, layer, seq, *, gmlp=None, tok=None, casts=()):
    rows = x2d.shape[0]
    bm = MIX_ROWS_GMLP if gmlp is not None else MIX_ROWS_ATTN
    seq_tiles = seq // bm
    qm_col = (z.shape[1] - MEM_WIDTH) // MEM_WIDTH
    shared_specs = [
        pl.BlockSpec((bm, MEM_WIDTH), lambda i: (i, qm_col)),
    ]
    tail_specs = [
        pl.BlockSpec((1, HEAD_DIM), lambda i: (0, 0)),
        pl.BlockSpec((None, None, MEM_WIDTH, N_MEM), lambda i: (layer, i // seq_tiles, 0, 0)),
        pl.BlockSpec((None, None, N_MEM, MEM_WIDTH), lambda i: (layer, i // seq_tiles, 0, 0)),
        pl.BlockSpec((D_MODEL, D_MODEL), lambda i: (0, 0), pipeline_mode=pl.Buffered(1)),
        pl.BlockSpec((bm, D_MODEL), lambda i: (i, 0)),
    ]
    tail_args = [g_mq.reshape(1, HEAD_DIM), mem_kt, mem_v, w_out, x2d]
    if gmlp is not None:
        g_v, w_s, b_s, gmlp_layer = gmlp
        body = _mix_out_gmlp_kernel
        in_specs = [
            pl.BlockSpec((bm, TOK_WIDTH), lambda i: (i, 0)),
            pl.BlockSpec((bm, TOK_WIDTH), lambda i: (i, 1)),
        ] + shared_specs + [
            pl.BlockSpec((1, TOK_WIDTH), lambda i: (0, 0)),
            pl.BlockSpec((None, A_GROUPS, CHUNK, CHUNK), lambda i: (gmlp_layer, 0, 0, 0)),
            pl.BlockSpec((A_GROUPS, CHUNK, 1), lambda i: (0, 0, 0)),
        ] + tail_specs
        args = [z, z, z, g_v.reshape(1, TOK_WIDTH), w_s, b_s.reshape(A_GROUPS, CHUNK, 1)] + tail_args
        name = "mix_out_gmlp"
    else:
        body = _mix_out_attn_kernel
        in_specs = [pl.BlockSpec((bm, TOK_WIDTH), lambda i: (i, 0))] + shared_specs + tail_specs
        args = [tok, z] + tail_args
        name = "mix_out_attn"
    cast_specs = [_weight_cast_specs(*cast, rows // bm, lambda i: i) for cast in casts]
    return pl.pallas_call(
        _with_weight_casts(body, len(in_specs), 1, len(casts)),
        grid=(rows // bm,),
        in_specs=in_specs + [src for src, _, _ in cast_specs],
        out_specs=[pl.BlockSpec((bm, D_MODEL), lambda i: (i, 0))] + [dst for _, dst, _ in cast_specs],
        out_shape=[jax.ShapeDtypeStruct((rows, D_MODEL), F32)] + [shape for _, _, shape in cast_specs],
        scratch_shapes=[pltpu.VMEM((bm, D_MODEL), BF16)],
        compiler_params=_params("parallel"),
        name=name,
    )(*args, *[cast[0] for cast in casts])


def _ffn_kernel(x_ref, g_ref, wg_ref, wu_ref, wd_ref, o_ref, h_ref):
    j = pl.program_id(1)

    def contribution(h):
        gate = _dot(h, wg_ref[...])
        up = _dot(h, wu_ref[...])
        act = (jax.nn.silu(gate) * up).astype(BF16)
        return _dot(act, wd_ref[...])

    @pl.when(j == 0)
    def _():
        for rows, h in _norm_row_chunks(x_ref, g_ref, h_ref):
            o_ref[rows, :] = x_ref[rows, :] + contribution(h)

    @pl.when(j > 0)
    def _():
        o_ref[...] += contribution(h_ref[...])


def _ffn(x2d, g, w_gate_up, w_down):
    rows = x2d.shape[0]
    d_ff = w_down.shape[0]
    bm, bf = FFN_ROWS, FFN_COLS
    ff_tiles = d_ff // bf
    return pl.pallas_call(
        _ffn_kernel,
        grid=(rows // bm, ff_tiles),
        in_specs=[
            pl.BlockSpec((bm, D_MODEL), lambda i, j: (i, 0)),
            pl.BlockSpec((1, D_MODEL), lambda i, j: (0, 0)),
            pl.BlockSpec((D_MODEL, bf), lambda i, j: (0, j)),
            pl.BlockSpec((D_MODEL, bf), lambda i, j: (0, ff_tiles + j)),
            pl.BlockSpec((bf, D_MODEL), lambda i, j: (j, 0)),
        ],
        out_specs=pl.BlockSpec((bm, D_MODEL), lambda i, j: (i, 0)),
        out_shape=jax.ShapeDtypeStruct((rows, D_MODEL), F32),
        scratch_shapes=[pltpu.VMEM((bm, D_MODEL), BF16)],
        compiler_params=_params("parallel", "arbitrary"),
        name="ffn",
    )(x2d, g.reshape(1, D_MODEL), w_gate_up, w_gate_up, w_down)


def _pair_layout_columns(a):
    lead = a.shape[:-1]
    n = len(lead)
    axes = tuple(range(n))
    half_pairs = KV_HEADS // 2
    q = a[..., :TOK_WIDTH].reshape(lead + (half_pairs, 2, Q_PER_KV, HEAD_DIM))
    q = q.transpose(axes + (n, n + 2, n + 1, n + 3)).reshape(lead + (half_pairs * Q_PER_KV, 2, HEAD_DIM))
    k = a[..., TOK_WIDTH:].reshape(lead + (half_pairs, 2, HEAD_DIM))
    blocks = jnp.concatenate([q, k], axis=n)
    blocks = blocks.reshape(lead + (blocks.shape[n], 2, 2, 2, ROPE_PAIRS))
    blocks = blocks.transpose(axes + (n, n + 3, n + 1, n + 2, n + 4))
    return blocks.reshape(lead + (TOK_WIDTH + KV_WIDTH,))


def _attn_tables(seq):
    n_rows = seq // GRID_W
    rows = jnp.broadcast_to(jnp.arange(n_rows)[:, None], (n_rows, GRID_W)).reshape(seq)
    cols = jnp.broadcast_to(jnp.arange(GRID_W)[None, :], (n_rows, GRID_W)).reshape(seq)
    freqs = ROPE_THETA ** (-jnp.arange(ROPE_PAIRS, dtype=F32) / ROPE_PAIRS)
    ang_r = rows.astype(F32)[:, None] * freqs
    ang_c = cols.astype(F32)[:, None] * freqs
    ang = jnp.concatenate([ang_r, ang_c, ang_r, ang_c], axis=-1)
    k_seg = (jnp.arange(2 * HEAD_DIM) % HEAD_DIM) // (HEAD_DIM // 2)
    n_seg = jnp.arange(HEAD_DIM) // (HEAD_DIM // 2)
    seg = (k_seg[:, None] == n_seg[None, :]).astype(BF16)
    d = jnp.arange(HEAD_DIM)
    axis, half, pair_index = d // (2 * ROPE_PAIRS), (d // ROPE_PAIRS) % 2, d % ROPE_PAIRS
    target = half[None, :] * HEAD_DIM + jnp.arange(2)[:, None] * (HEAD_DIM // 2) + axis[None, :] * ROPE_PAIRS \
        + pair_index[None, :]
    perm = (target[:, :, None] == jnp.arange(2 * HEAD_DIM)[None, None, :]).astype(BF16)
    return jnp.cos(ang), jnp.sin(ang), seg, perm


def _attn_gains(g_q, g_k, cols):
    gains = jnp.concatenate([jnp.tile(g_q * (SCORE_SCALE * LOG2_E), Q_HEADS), jnp.tile(g_k, KV_HEADS)])
    return jnp.concatenate([_pair_layout_columns(gains),
                            jnp.ones((cols - TOK_WIDTH - KV_WIDTH,), F32)]).reshape(1, cols)


def kernel(x, mem, g_mix, g_ffn, w_in_a, g_v_a, w_spatial, b_spatial, w_in_b, g_q_b, g_k_b, g_mem, w_mem_kv,
           g_mq, g_mk, w_out, w_gate_up, w_down):
    batch, seq, _ = x.shape
    rows = batch * seq
    assert seq % IN_PROJ_ROWS == 0 and seq % ATTN_Q_ROWS == 0 and rows % FFN_ROWS == 0
    assert seq % MIX_ROWS_GMLP == 0 and seq % MIX_ROWS_ATTN == 0 and MIX_ROWS_GMLP % CHUNK == 0
    assert w_down.shape[1] % FFN_COLS == 0

    cos, sin, seg, perm = _attn_tables(seq)
    mem_kt, mem_v = _mem_kv(mem, g_mem, w_mem_kv, g_mk)
    w_in_a, w_spatial = w_in_a.astype(BF16), w_spatial.astype(BF16)
    xs = x.reshape(rows, D_MODEL)
    for l in range(DEPTH):
        idx = l // 2
        ffn_casts = ((w_gate_up, l), (w_down, l))
        if l % 2 == 0:
            z, w_out_l, w_out_next, w_in_next = _in_proj(xs, g_mix[l], w_in_a, idx,
                                                         casts=((w_out, l), (w_out, l + 1), (w_in_b, idx)))
            xs, ffn_gate_up, ffn_down = _mix_out(xs, z, mem_kt, mem_v, g_mq[l], w_out_l, l, seq,
                                                 gmlp=(g_v_a[idx], w_spatial, b_spatial[idx], idx),
                                                 casts=ffn_casts)
        else:
            gain = _attn_gains(g_q_b[idx], g_k_b[idx], w_in_next.shape[1])
            z, = _in_proj(xs, g_mix[l], w_in_next, None, attn=(gain, cos, sin, seg, perm))
            tok, ffn_gate_up, ffn_down = _attention(z, batch, seq, casts=ffn_casts)
            xs, = _mix_out(xs, z, mem_kt, mem_v, g_mq[l], w_out_next, l, seq, tok=tok)
        xs = _ffn(xs, g_ffn[l], ffn_gate_up, ffn_down)
    return xs.reshape(batch, seq, D_MODEL)
```

```python
import functools

import jax
import jax.numpy as jnp
from jax import lax
from jax.experimental import pallas as pl
from jax.experimental.pallas import tpu as pltpu

D_MODEL = 2048
DEPTH = 4
N_MEM = 256
GRID_W = 64
HEAD_DIM = 128
MEM_HEADS = 4
MEM_WIDTH = MEM_HEADS * HEAD_DIM
TOK_WIDTH = D_MODEL - MEM_WIDTH
CHUNK = 128
A_GROUPS = TOK_WIDTH // HEAD_DIM
Q_HEADS = TOK_WIDTH // HEAD_DIM
KV_HEADS = 4
Q_PER_KV = Q_HEADS // KV_HEADS
KV_WIDTH = KV_HEADS * HEAD_DIM
ROPE_THETA = 10000.0
ROPE_PAIRS = HEAD_DIM // 4
EPS = 1e-6
SCORE_SCALE = HEAD_DIM ** -0.5

VMEM_LIMIT_BYTES = 63 * 1024 * 1024
MXU_COLS = 256
LOG2_E = 1.4426950408889634

IN_PROJ_ROWS = 1024
IN_PROJ_DOT_COLS = 512
MIX_ROWS_GMLP = 512
MIX_ROWS_ATTN = 1024
ATTN_Q_ROWS = 2048
ATTN_CHAIN_ROWS = 256
FFN_ROWS = 1024
FFN_COLS = 512
NORM_CHUNK_ROWS = 256

BF16 = jnp.bfloat16
F32 = jnp.float32


def _rms_scale(x):
    return lax.rsqrt(jnp.mean(x * x, axis=-1, keepdims=True) + EPS)


def _gelu_exact(x):
    return 0.5 * x * (1.0 + lax.erf(x * (2.0 ** -0.5)))


def _dot(a, b):
    return jnp.dot(a, b, preferred_element_type=F32)


def _params(*semantics):
    return pltpu.CompilerParams(dimension_semantics=semantics, vmem_limit_bytes=VMEM_LIMIT_BYTES)


BF16_TILE_ROWS = 16
LANES = 128


def _weight_cast_specs(w, layer, n_steps, step_id):
    _, rows, cols = w.shape
    n_col = 1 if rows % (BF16_TILE_ROWS * n_steps) == 0 else 2
    n_row = n_steps // n_col
    assert n_row * n_col == n_steps and rows % (BF16_TILE_ROWS * n_row) == 0 and cols % (LANES * n_col) == 0
    block = (rows // n_row, cols // n_col)
    src = pl.BlockSpec((None,) + block, lambda *g: (layer, step_id(*g) // n_col, step_id(*g) % n_col))
    dst = pl.BlockSpec(block, lambda *g: (step_id(*g) // n_col, step_id(*g) % n_col))
    return src, dst, jax.ShapeDtypeStruct((rows, cols), BF16)


def _with_weight_casts(body, n_in, n_out, n_cast):
    def kernel(*refs):
        ins, refs = refs[:n_in], refs[n_in:]
        srcs, refs = refs[:n_cast], refs[n_cast:]
        outs, refs = refs[:n_out], refs[n_out:]
        dsts, scratch = refs[:n_cast], refs[n_cast:]
        for src, dst in zip(srcs, dsts):
            dst[...] = src[...].astype(BF16)
        body(*ins, *outs, *scratch)
    return kernel


def _mem_kv_kernel(mem_ref, g_mem_ref, w_ref, g_mk_ref, kt_ref, v_ref):
    m = mem_ref[...]
    h = (m * _rms_scale(m) * g_mem_ref[...]).astype(BF16)
    kv = _dot(h, w_ref[...].astype(BF16))
    for hd in range(MEM_HEADS):
        k = kv[:, hd * HEAD_DIM:(hd + 1) * HEAD_DIM]
        k = k * _rms_scale(k) * g_mk_ref[...]
        kt_ref[hd * HEAD_DIM:(hd + 1) * HEAD_DIM, :] = k.T.astype(BF16)
    v_ref[...] = kv[:, MEM_WIDTH:].astype(BF16)


def _mem_kv(mem, g_mem, w_mem_kv, g_mk):
    batch = mem.shape[0]
    return pl.pallas_call(
        _mem_kv_kernel,
        grid=(DEPTH, batch),
        in_specs=[
            pl.BlockSpec((None, N_MEM, D_MODEL), lambda l, b: (b, 0, 0)),
            pl.BlockSpec((None, 1, D_MODEL), lambda l, b: (l, 0, 0)),
            pl.BlockSpec((None, D_MODEL, 2 * MEM_WIDTH), lambda l, b: (l, 0, 0)),
            pl.BlockSpec((None, 1, HEAD_DIM), lambda l, b: (l, 0, 0)),
        ],
        out_specs=[
            pl.BlockSpec((None, None, MEM_WIDTH, N_MEM), lambda l, b: (l, b, 0, 0)),
            pl.BlockSpec((None, None, N_MEM, MEM_WIDTH), lambda l, b: (l, b, 0, 0)),
        ],
        out_shape=[
            jax.ShapeDtypeStruct((DEPTH, batch, MEM_WIDTH, N_MEM), BF16),
            jax.ShapeDtypeStruct((DEPTH, batch, N_MEM, MEM_WIDTH), BF16),
        ],
        compiler_params=_params("arbitrary", "arbitrary"),
        name="mem_kv",
    )(mem, g_mem.reshape(DEPTH, 1, D_MODEL), w_mem_kv, g_mk.reshape(DEPTH, 1, HEAD_DIM))


def _norm_row_chunks(x_ref, g_ref, h_ref):
    for r in range(0, x_ref.shape[0], NORM_CHUNK_ROWS):
        rows = slice(r, r + NORM_CHUNK_ROWS)
        x = x_ref[rows, :]
        h = (x * _rms_scale(x) * g_ref[...]).astype(BF16)
        h_ref[rows, :] = h
        yield rows, h


def _project_column_chunks(x_ref, g_ref, h_ref, n_cols, project):
    for rows, h in _norm_row_chunks(x_ref, g_ref, h_ref):
        project(h, rows, 0)
    h = h_ref[...]
    for c0 in range(IN_PROJ_DOT_COLS, n_cols, IN_PROJ_DOT_COLS):
        project(h, slice(0, x_ref.shape[0]), c0)


def _in_proj_gmlp_kernel(x_ref, g_ref, w_ref, z_ref, h_ref, *, gelu_cols):
    def project(h, rows, c0):
        cols = slice(c0, c0 + IN_PROJ_DOT_COLS)
        acc = _dot(h, w_ref[:, cols])
        z_ref[rows, cols] = (_gelu_exact(acc) if c0 < gelu_cols else acc).astype(BF16)

    _project_column_chunks(x_ref, g_ref, h_ref, z_ref.shape[1], project)


def _pair_block_heads():
    blocks = [((2 * pair) * Q_PER_KV + g, (2 * pair + 1) * Q_PER_KV + g)
              for pair in range(KV_HEADS // 2) for g in range(Q_PER_KV)]
    return blocks + [(Q_HEADS + 2 * pair, Q_HEADS + 2 * pair + 1) for pair in range(KV_HEADS // 2)]


def _in_proj_attn_kernel(x_ref, g_ref, w_ref, gain_ref, cos_ref, sin_ref, seg_ref, perm_ref, z_ref, h_ref, wp_ref,
                         *, rope_cols):
    @pl.when(pl.program_id(0) == 0)
    def _():
        for blk, heads in enumerate(_pair_block_heads()):
            scattered = [_dot(w_ref[:, hd * HEAD_DIM:(hd + 1) * HEAD_DIM], perm_ref[m]) for m, hd in enumerate(heads)]
            wp_ref[:, blk * 2 * HEAD_DIM:(blk + 1) * 2 * HEAD_DIM] = (scattered[0] + scattered[1]).astype(BF16)

    def project(h, rows, c0):
        w_cols = wp_ref if c0 < rope_cols else w_ref
        acc = _dot(h, w_cols[:, c0:c0 + IN_PROJ_DOT_COLS])
        if c0 >= rope_cols:
            z_ref[rows, c0:c0 + IN_PROJ_DOT_COLS] = acc.astype(BF16)
            return
        cos, sin = cos_ref[rows, :], sin_ref[rows, :]
        for p0 in range(0, IN_PROJ_DOT_COLS, 2 * HEAD_DIM):
            t0, t1 = acc[:, p0:p0 + HEAD_DIM], acc[:, p0 + HEAD_DIM:p0 + 2 * HEAD_DIM]
            lanes0 = slice(c0 + p0, c0 + p0 + HEAD_DIM)
            lanes1 = slice(c0 + p0 + HEAD_DIM, c0 + p0 + 2 * HEAD_DIM)
            ss = t0 * t0 + t1 * t1
            hi = ss.astype(BF16)
            lo = (ss - hi.astype(F32)).astype(BF16)
            tot = _dot(jnp.concatenate([hi, lo], axis=1), seg_ref[...])
            r = lax.rsqrt(tot * (1.0 / HEAD_DIM) + EPS)
            n0 = t0 * r * gain_ref[:, lanes0]
            n1 = t1 * r * gain_ref[:, lanes1]
            z_ref[rows, lanes0] = (n0 * cos - n1 * sin).astype(BF16)
            z_ref[rows, lanes1] = (n1 * cos + n0 * sin).astype(BF16)

    _project_column_chunks(x_ref, g_ref, h_ref, z_ref.shape[1], project)


def _in_proj(x2d, g, w, layer, *, attn=None, casts=()):
    rows, _ = x2d.shape
    cols = w.shape[-1]
    bm = IN_PROJ_ROWS
    assert cols % IN_PROJ_DOT_COLS == 0
    resident = pl.Buffered(1)
    if layer is None:
        w_spec = pl.BlockSpec((D_MODEL, cols), lambda i: (0, 0), pipeline_mode=resident)
    else:
        w_spec = pl.BlockSpec((None, D_MODEL, cols), lambda i: (layer, 0, 0), pipeline_mode=resident)
    in_specs = [
        pl.BlockSpec((bm, D_MODEL), lambda i: (i, 0)),
        pl.BlockSpec((1, D_MODEL), lambda i: (0, 0)),
        w_spec,
    ]
    args = [x2d, g.reshape(1, D_MODEL), w]
    scratch_shapes = [pltpu.VMEM((bm, D_MODEL), BF16)]
    if attn is None:
        assert (2 * TOK_WIDTH) % IN_PROJ_DOT_COLS == 0
        body = functools.partial(_in_proj_gmlp_kernel, gelu_cols=2 * TOK_WIDTH)
    else:
        gain, cos, sin, seg, perm = attn
        seq_tiles = cos.shape[0] // bm
        rope_cols = TOK_WIDTH + KV_WIDTH
        assert rope_cols % IN_PROJ_DOT_COLS == 0
        body = functools.partial(_in_proj_attn_kernel, rope_cols=rope_cols)
        in_specs += [
            pl.BlockSpec((1, cols), lambda i: (0, 0)),
            pl.BlockSpec((bm, HEAD_DIM), lambda i: (i % seq_tiles, 0)),
            pl.BlockSpec((bm, HEAD_DIM), lambda i: (i % seq_tiles, 0)),
            pl.BlockSpec((2 * HEAD_DIM, HEAD_DIM), lambda i: (0, 0)),
            pl.BlockSpec((2, HEAD_DIM, 2 * HEAD_DIM), lambda i: (0, 0, 0)),
        ]
        args += [gain, cos, sin, seg, perm]
        scratch_shapes.append(pltpu.VMEM((D_MODEL, rope_cols), BF16))
    cast_specs = [_weight_cast_specs(*cast, rows // bm, lambda i: i) for cast in casts]
    return pl.pallas_call(
        _with_weight_casts(body, len(in_specs), 1, len(casts)),
        grid=(rows // bm,),
        in_specs=in_specs + [src for src, _, _ in cast_specs],
        out_specs=[pl.BlockSpec((bm, cols), lambda i: (i, 0))] + [dst for _, dst, _ in cast_specs],
        out_shape=[jax.ShapeDtypeStruct((rows, cols), BF16)] + [shape for _, _, shape in cast_specs],
        scratch_shapes=scratch_shapes,
        compiler_params=_params("arbitrary"),
        name="in_proj_gmlp" if attn is None else "in_proj_attn",
    )(*args, *[cast[0] for cast in casts])


_NT_DIMS = (((1,), (1,)), ((), ()))


def _attention_kernel(q_ref, k_ref, v_ref, o_ref, v1_ref):
    @pl.when(pl.program_id(2) == 0)
    def _():
        ones_col = lax.broadcasted_iota(jnp.int32, (v_ref.shape[0], HEAD_DIM), 1) == 0
        v1_ref[:, :HEAD_DIM] = v_ref[...]
        v1_ref[:, HEAD_DIM:] = jnp.where(ones_col, 1.0, 0.0).astype(BF16)

    k = k_ref[...]
    v = v1_ref[...]
    half = HEAD_DIM // 2
    lane = lax.broadcasted_iota(jnp.int32, (ATTN_CHAIN_ROWS, 2 * HEAD_DIM), 1) % HEAD_DIM
    first_lane = (pl.program_id(1) % 2) * half
    keep = jnp.logical_and(lane >= first_lane, lane < first_lane + half)
    for r in range(0, q_ref.shape[0], ATTN_CHAIN_ROWS):
        rows = slice(r, r + ATTN_CHAIN_ROWS)
        for g in range(Q_PER_KV):
            q = q_ref[rows, g * 2 * HEAD_DIM:(g + 1) * 2 * HEAD_DIM]
            q = jnp.where(keep, q, jnp.zeros_like(q))
            s = lax.dot_general(q, k, _NT_DIMS, preferred_element_type=F32)
            p = jnp.exp2((s - jnp.max(s, axis=-1, keepdims=True)).astype(BF16))
            o = _dot(p, v)
            o = o[:, :HEAD_DIM] / o[:, HEAD_DIM:HEAD_DIM + 1]
            o_ref[rows, g * HEAD_DIM:(g + 1) * HEAD_DIM] = o.astype(BF16)


def _attention(z, batch, seq, casts=()):
    bq = ATTN_Q_ROWS
    q_tiles = seq // bq
    pair_width = 2 * HEAD_DIM
    k_block0 = TOK_WIDTH // pair_width
    v_block0 = (TOK_WIDTH + KV_WIDTH) // HEAD_DIM
    grid = (batch, KV_HEADS, q_tiles)
    cast_specs = [_weight_cast_specs(*cast, batch * KV_HEADS * q_tiles,
                                     lambda b, h, i: (b * KV_HEADS + h) * q_tiles + i) for cast in casts]
    in_specs = [
        pl.BlockSpec((bq, Q_PER_KV * pair_width), lambda b, h, i: (b * q_tiles + i, h // 2)),
        pl.BlockSpec((seq, pair_width), lambda b, h, i: (b, k_block0 + h // 2)),
        pl.BlockSpec((seq, HEAD_DIM), lambda b, h, i: (b, v_block0 + h)),
    ]
    return pl.pallas_call(
        _with_weight_casts(_attention_kernel, len(in_specs), 1, len(casts)),
        grid=grid,
        in_specs=in_specs + [src for src, _, _ in cast_specs],
        out_specs=[pl.BlockSpec((bq, Q_PER_KV * HEAD_DIM), lambda b, h, i: (b * q_tiles + i, h))]
        + [dst for _, dst, _ in cast_specs],
        out_shape=[jax.ShapeDtypeStruct((batch * seq, TOK_WIDTH), BF16)] + [shape for _, _, shape in cast_specs],
        scratch_shapes=[pltpu.VMEM((seq, 2 * HEAD_DIM), BF16)],
        compiler_params=_params("parallel", "parallel", "arbitrary"),
        name="attention",
    )(z, z, z, *[cast[0] for cast in casts])


def _memory_attention_into(mixed_ref, qm_ref, g_mq_ref, kt_ref, v_ref):
    for hd in range(MEM_HEADS):
        lanes = slice(hd * HEAD_DIM, (hd + 1) * HEAD_DIM)
        q = qm_ref[:, lanes].astype(F32)
        q = (q * _rms_scale(q) * g_mq_ref[...] * SCORE_SCALE).astype(BF16)
        s = _dot(q, kt_ref[lanes, :])
        p = jnp.exp(s - jnp.max(s, axis=-1, keepdims=True))
        denom = jnp.sum(p, axis=-1, keepdims=True)
        o = _dot(p.astype(BF16), v_ref[:, lanes]) / denom
        mixed_ref[:, TOK_WIDTH + hd * HEAD_DIM:TOK_WIDTH + (hd + 1) * HEAD_DIM] = o.astype(BF16)


def _mix_out_gmlp_kernel(u_ref, v_ref, qm_ref, g_v_ref, ws_ref, bs_ref, g_mq_ref, kt_ref, mv_ref,
                         w_out_ref, x_ref, o_ref, mixed_ref):
    v = v_ref[...].astype(F32)
    vn = (v * _rms_scale(v) * g_v_ref[...]).astype(BF16)
    for c in range(v.shape[0] // CHUNK):
        rows = slice(c * CHUNK, (c + 1) * CHUNK)
        for g in range(A_GROUPS):
            lanes = slice(g * HEAD_DIM, (g + 1) * HEAD_DIM)
            s = _dot(ws_ref[g], vn[rows, lanes]) + bs_ref[g]
            mixed_ref[rows, lanes] = (u_ref[rows, lanes].astype(F32) * s).astype(BF16)
    _memory_attention_into(mixed_ref, qm_ref, g_mq_ref, kt_ref, mv_ref)
    o_ref[...] = x_ref[...] + _dot(mixed_ref[...], w_out_ref[...])


def _mix_out_attn_kernel(t_ref, qm_ref, g_mq_ref, kt_ref, mv_ref, w_out_ref, x_ref, o_ref, mixed_ref):
    mixed_ref[:, :TOK_WIDTH] = t_ref[...]
    _memory_attention_into(mixed_ref, qm_ref, g_mq_ref, kt_ref, mv_ref)
    o_ref[...] = x_ref[...] + _dot(mixed_ref[...], w_out_ref[...])


def _mix_out(x2d, z, mem_kt, mem_v, g_mq, w_out, layer, seq, *, gmlp=None, tok=None, casts=()):
    rows = x2d.shape[0]
    bm = MIX_ROWS_GMLP if gmlp is not None else MIX_ROWS_ATTN
    seq_tiles = seq // bm
    qm_col = (z.shape[1] - MEM_WIDTH) // MEM_WIDTH
    shared_specs = [
        pl.BlockSpec((bm, MEM_WIDTH), lambda i: (i, qm_col)),
    ]
    tail_specs = [
        pl.BlockSpec((1, HEAD_DIM), lambda i: (0, 0)),
        pl.BlockSpec((None, None, MEM_WIDTH, N_MEM), lambda i: (layer, i // seq_tiles, 0, 0)),
        pl.BlockSpec((None, None, N_MEM, MEM_WIDTH), lambda i: (layer, i // seq_tiles, 0, 0)),
        pl.BlockSpec((D_MODEL, D_MODEL), lambda i: (0, 0), pipeline_mode=pl.Buffered(1)),
        pl.BlockSpec((bm, D_MODEL), lambda i: (i, 0)),
    ]
    tail_args = [g_mq.reshape(1, HEAD_DIM), mem_kt, mem_v, w_out, x2d]
    if gmlp is not None:
        g_v, w_s, b_s, gmlp_layer = gmlp
        body = _mix_out_gmlp_kernel
        in_specs = [
            pl.BlockSpec((bm, TOK_WIDTH), lambda i: (i, 0)),
            pl.BlockSpec((bm, TOK_WIDTH), lambda i: (i, 1)),
        ] + shared_specs + [
            pl.BlockSpec((1, TOK_WIDTH), lambda i: (0, 0)),
            pl.BlockSpec((None, A_GROUPS, CHUNK, CHUNK), lambda i: (gmlp_layer, 0, 0, 0)),
            pl.BlockSpec((A_GROUPS, CHUNK, 1), lambda i: (0, 0, 0)),
        ] + tail_specs
        args = [z, z, z, g_v.reshape(1, TOK_WIDTH), w_s, b_s.reshape(A_GROUPS, CHUNK, 1)] + tail_args
        name = "mix_out_gmlp"
    else:
        body = _mix_out_attn_kernel
        in_specs = [pl.BlockSpec((bm, TOK_WIDTH), lambda i: (i, 0))] + shared_specs + tail_specs
        args = [tok, z] + tail_args
        name = "mix_out_attn"
    cast_specs = [_weight_cast_specs(*cast, rows // bm, lambda i: i) for cast in casts]
    return pl.pallas_call(
        _with_weight_casts(body, len(in_specs), 1, len(casts)),
        grid=(rows // bm,),
        in_specs=in_specs + [src for src, _, _ in cast_specs],
        out_specs=[pl.BlockSpec((bm, D_MODEL), lambda i: (i, 0))] + [dst for _, dst, _ in cast_specs],
        out_shape=[jax.ShapeDtypeStruct((rows, D_MODEL), F32)] + [shape for _, _, shape in cast_specs],
        scratch_shapes=[pltpu.VMEM((bm, D_MODEL), BF16)],
        compiler_params=_params("parallel"),
        name=name,
    )(*args, *[cast[0] for cast in casts])


def _ffn_kernel(x_ref, g_ref, wg_ref, wu_ref, wd_ref, o_ref, h_ref):
    j = pl.program_id(1)

    def contribution(h):
        gate = _dot(h, wg_ref[...])
        up = _dot(h, wu_ref[...])
        act = (jax.nn.silu(gate) * up).astype(BF16)
        return _dot(act, wd_ref[...])

    @pl.when(j == 0)
    def _():
        for rows, h in _norm_row_chunks(x_ref, g_ref, h_ref):
            o_ref[rows, :] = x_ref[rows, :] + contribution(h)

    @pl.when(j > 0)
    def _():
        o_ref[...] += contribution(h_ref[...])


def _ffn(x2d, g, w_gate_up, w_down):
    rows = x2d.shape[0]
    d_ff = w_down.shape[0]
    bm, bf = FFN_ROWS, FFN_COLS
    ff_tiles = d_ff // bf
    return pl.pallas_call(
        _ffn_kernel,
        grid=(rows // bm, ff_tiles),
        in_specs=[
            pl.BlockSpec((bm, D_MODEL), lambda i, j: (i, 0)),
            pl.BlockSpec((1, D_MODEL), lambda i, j: (0, 0)),
            pl.BlockSpec((D_MODEL, bf), lambda i, j: (0, j)),
            pl.BlockSpec((D_MODEL, bf), lambda i, j: (0, ff_tiles + j)),
            pl.BlockSpec((bf, D_MODEL), lambda i, j: (j, 0)),
        ],
        out_specs=pl.BlockSpec((bm, D_MODEL), lambda i, j: (i, 0)),
        out_shape=jax.ShapeDtypeStruct((rows, D_MODEL), F32),
        scratch_shapes=[pltpu.VMEM((bm, D_MODEL), BF16)],
        compiler_params=_params("parallel", "arbitrary"),
        name="ffn",
    )(x2d, g.reshape(1, D_MODEL), w_gate_up, w_gate_up, w_down)


def _pair_layout_columns(a):
    lead = a.shape[:-1]
    n = len(lead)
    axes = tuple(range(n))
    half_pairs = KV_HEADS // 2
    q = a[..., :TOK_WIDTH].reshape(lead + (half_pairs, 2, Q_PER_KV, HEAD_DIM))
    q = q.transpose(axes + (n, n + 2, n + 1, n + 3)).reshape(lead + (half_pairs * Q_PER_KV, 2, HEAD_DIM))
    k = a[..., TOK_WIDTH:].reshape(lead + (half_pairs, 2, HEAD_DIM))
    blocks = jnp.concatenate([q, k], axis=n)
    blocks = blocks.reshape(lead + (blocks.shape[n], 2, 2, 2, ROPE_PAIRS))
    blocks = blocks.transpose(axes + (n, n + 3, n + 1, n + 2, n + 4))
    return blocks.reshape(lead + (TOK_WIDTH + KV_WIDTH,))


def _attn_tables(seq):
    n_rows = seq // GRID_W
    rows = jnp.broadcast_to(jnp.arange(n_rows)[:, None], (n_rows, GRID_W)).reshape(seq)
    cols = jnp.broadcast_to(jnp.arange(GRID_W)[None, :], (n_rows, GRID_W)).reshape(seq)
    freqs = ROPE_THETA ** (-jnp.arange(ROPE_PAIRS, dtype=F32) / ROPE_PAIRS)
    ang_r = rows.astype(F32)[:, None] * freqs
    ang_c = cols.astype(F32)[:, None] * freqs
    ang = jnp.concatenate([ang_r, ang_c, ang_r, ang_c], axis=-1)
    k_seg = (jnp.arange(2 * HEAD_DIM) % HEAD_DIM) // (HEAD_DIM // 2)
    n_seg = jnp.arange(HEAD_DIM) // (HEAD_DIM // 2)
    seg = (k_seg[:, None] == n_seg[None, :]).astype(BF16)
    d = jnp.arange(HEAD_DIM)
    axis, half, pair_index = d // (2 * ROPE_PAIRS), (d // ROPE_PAIRS) % 2, d % ROPE_PAIRS
    target = half[None, :] * HEAD_DIM + jnp.arange(2)[:, None] * (HEAD_DIM // 2) + axis[None, :] * ROPE_PAIRS \
        + pair_index[None, :]
    perm = (target[:, :, None] == jnp.arange(2 * HEAD_DIM)[None, None, :]).astype(BF16)
    return jnp.cos(ang), jnp.sin(ang), seg, perm


def _attn_gains(g_q, g_k, cols):
    gains = jnp.concatenate([jnp.tile(g_q * (SCORE_SCALE * LOG2_E), Q_HEADS), jnp.tile(g_k, KV_HEADS)])
    return jnp.concatenate([_pair_layout_columns(gains),
                            jnp.ones((cols - TOK_WIDTH - KV_WIDTH,), F32)]).reshape(1, cols)


def kernel(x, mem, g_mix, g_ffn, w_in_a, g_v_a, w_spatial, b_spatial, w_in_b, g_q_b, g_k_b, g_mem, w_mem_kv,
           g_mq, g_mk, w_out, w_gate_up, w_down):
    batch, seq, _ = x.shape
    rows = batch * seq
    assert seq % IN_PROJ_ROWS == 0 and seq % ATTN_Q_ROWS == 0 and rows % FFN_ROWS == 0
    assert seq % MIX_ROWS_GMLP == 0 and seq % MIX_ROWS_ATTN == 0 and MIX_ROWS_GMLP % CHUNK == 0
    assert w_down.shape[1] % FFN_COLS == 0

    cos, sin, seg, perm = _attn_tables(seq)
    mem_kt, mem_v = _mem_kv(mem, g_mem, w_mem_kv, g_mk)
    w_in_a, w_spatial = w_in_a.astype(BF16), w_spatial.astype(BF16)
    xs = x.reshape(rows, D_MODEL)
    for l in range(DEPTH):
        idx = l // 2
        ffn_casts = ((w_gate_up, l), (w_down, l))
        if l % 2 == 0:
            z, w_out_l, w_out_next, w_in_next = _in_proj(xs, g_mix[l], w_in_a, idx,
                                                         casts=((w_out, l), (w_out, l + 1), (w_in_b, idx)))
            xs, ffn_gate_up, ffn_down = _mix_out(xs, z, mem_kt, mem_v, g_mq[l], w_out_l, l, seq,
                                                 gmlp=(g_v_a[idx], w_spatial, b_spatial[idx], idx),
                                                 casts=ffn_casts)
        else:
            gain = _attn_gains(g_q_b[idx], g_k_b[idx], w_in_next.shape[1])
            z, = _in_proj(xs, g_mix[l], w_in_next, None, attn=(gain, cos, sin, seg, perm))
            tok, ffn_gate_up, ffn_down = _attention(z, batch, seq, casts=ffn_casts)
            xs, = _mix_out(xs, z, mem_kt, mem_v, g_mq[l], w_out_next, l, seq, tok=tok)
        xs = _ffn(xs, g_ffn[l], ffn_gate_up, ffn_down)
    return xs.reshape(batch, seq, D_MODEL)
```

```python
import functools

import jax
import jax.numpy as jnp
from jax import lax
from jax.experimental import pallas as pl
from jax.experimental.pallas import tpu as pltpu

D_MODEL = 2048
DEPTH = 4
N_MEM = 256
GRID_W = 64
HEAD_DIM = 128
MEM_HEADS = 4
MEM_WIDTH = MEM_HEADS * HEAD_DIM
TOK_WIDTH = D_MODEL - MEM_WIDTH
CHUNK = 128
A_GROUPS = TOK_WIDTH // HEAD_DIM
Q_HEADS = TOK_WIDTH // HEAD_DIM
KV_HEADS = 4
Q_PER_KV = Q_HEADS // KV_HEADS
KV_WIDTH = KV_HEADS * HEAD_DIM
ROPE_THETA = 10000.0
ROPE_PAIRS = HEAD_DIM // 4
EPS = 1e-6
SCORE_SCALE = HEAD_DIM ** -0.5

VMEM_LIMIT_BYTES = 63 * 1024 * 1024
MXU_COLS = 256
LOG2_E = 1.4426950408889634

IN_PROJ_ROWS = 1024
IN_PROJ_DOT_COLS = 2 * MXU_COLS
MIX_ROWS_GMLP = 512
MIX_ROWS_ATTN = 1024
ATTN_Q_ROWS = 2048
ATTN_CHAIN_ROWS = 256
FFN_ROWS = 1024
FFN_COLS = 512
NORM_CHUNK_ROWS = 256

BF16 = jnp.bfloat16
F32 = jnp.float32


def _rms_scale(x):
    return lax.rsqrt(jnp.mean(x * x, axis=-1, keepdims=True) + EPS)


def _gelu_exact(x):
    return 0.5 * x * (1.0 + lax.erf(x * (2.0 ** -0.5)))


def _dot(a, b):
    return jnp.dot(a, b, preferred_element_type=F32)


def _params(*semantics):
    return pltpu.CompilerParams(dimension_semantics=semantics, vmem_limit_bytes=VMEM_LIMIT_BYTES)


BF16_TILE_ROWS = 16
LANES = 128


def _weight_cast_specs(w, layer, n_steps, step_id):
    _, rows, cols = w.shape
    n_col = 1 if rows % (BF16_TILE_ROWS * n_steps) == 0 else 2
    n_row = n_steps // n_col
    assert n_row * n_col == n_steps and rows % (BF16_TILE_ROWS * n_row) == 0 and cols % (LANES * n_col) == 0
    block = (rows // n_row, cols // n_col)
    src = pl.BlockSpec((None,) + block, lambda *g: (layer, step_id(*g) // n_col, step_id(*g) % n_col))
    dst = pl.BlockSpec(block, lambda *g: (step_id(*g) // n_col, step_id(*g) % n_col))
    return src, dst, jax.ShapeDtypeStruct((rows, cols), BF16)


def _with_weight_casts(body, n_in, n_out, n_cast):
    def kernel(*refs):
        ins, refs = refs[:n_in], refs[n_in:]
        srcs, refs = refs[:n_cast], refs[n_cast:]
        outs, refs = refs[:n_out], refs[n_out:]
        dsts, scratch = refs[:n_cast], refs[n_cast:]
        for src, dst in zip(srcs, dsts):
            dst[...] = src[...].astype(BF16)
        body(*ins, *outs, *scratch)
    return kernel


def _mem_kv_kernel(mem_ref, g_mem_ref, w_ref, g_mk_ref, kt_ref, v_ref):
    m = mem_ref[...]
    h = (m * _rms_scale(m) * g_mem_ref[...]).astype(BF16)
    kv = _dot(h, w_ref[...].astype(BF16))
    for hd in range(MEM_HEADS):
        k = kv[:, hd * HEAD_DIM:(hd + 1) * HEAD_DIM]
        k = k * _rms_scale(k) * g_mk_ref[...]
        kt_ref[hd * HEAD_DIM:(hd + 1) * HEAD_DIM, :] = k.T.astype(BF16)
    v_ref[...] = kv[:, MEM_WIDTH:].astype(BF16)


def _mem_kv(mem, g_mem, w_mem_kv, g_mk):
    batch = mem.shape[0]
    return pl.pallas_call(
        _mem_kv_kernel,
        grid=(DEPTH, batch),
        in_specs=[
            pl.BlockSpec((None, N_MEM, D_MODEL), lambda l, b: (b, 0, 0)),
            pl.BlockSpec((None, 1, D_MODEL), lambda l, b: (l, 0, 0)),
            pl.BlockSpec((None, D_MODEL, 2 * MEM_WIDTH), lambda l, b: (l, 0, 0)),
            pl.BlockSpec((None, 1, HEAD_DIM), lambda l, b: (l, 0, 0)),
        ],
        out_specs=[
            pl.BlockSpec((None, None, MEM_WIDTH, N_MEM), lambda l, b: (l, b, 0, 0)),
            pl.BlockSpec((None, None, N_MEM, MEM_WIDTH), lambda l, b: (l, b, 0, 0)),
        ],
        out_shape=[
            jax.ShapeDtypeStruct((DEPTH, batch, MEM_WIDTH, N_MEM), BF16),
            jax.ShapeDtypeStruct((DEPTH, batch, N_MEM, MEM_WIDTH), BF16),
        ],
        compiler_params=_params("arbitrary", "arbitrary"),
        name="mem_kv",
    )(mem, g_mem.reshape(DEPTH, 1, D_MODEL), w_mem_kv, g_mk.reshape(DEPTH, 1, HEAD_DIM))


def _norm_row_chunks(x_ref, g_ref, h_ref):
    for r in range(0, x_ref.shape[0], NORM_CHUNK_ROWS):
        rows = slice(r, r + NORM_CHUNK_ROWS)
        x = x_ref[rows, :]
        h = (x * _rms_scale(x) * g_ref[...]).astype(BF16)
        h_ref[rows, :] = h
        yield rows, h


def _project_column_chunks(x_ref, g_ref, h_ref, n_cols, project):
    for rows, h in _norm_row_chunks(x_ref, g_ref, h_ref):
        project(h, rows, 0)
    h = h_ref[...]
    for c0 in range(IN_PROJ_DOT_COLS, n_cols, IN_PROJ_DOT_COLS):
        project(h, slice(0, x_ref.shape[0]), c0)


def _in_proj_gmlp_kernel(x_ref, g_ref, w_ref, z_ref, h_ref, *, gelu_cols):
    def project(h, rows, c0):
        cols = slice(c0, c0 + IN_PROJ_DOT_COLS)
        acc = _dot(h, w_ref[:, cols])
        z_ref[rows, cols] = (_gelu_exact(acc) if c0 < gelu_cols else acc).astype(BF16)

    _project_column_chunks(x_ref, g_ref, h_ref, z_ref.shape[1], project)


def _pair_block_heads():
    blocks = [((2 * pair) * Q_PER_KV + g, (2 * pair + 1) * Q_PER_KV + g)
              for pair in range(KV_HEADS // 2) for g in range(Q_PER_KV)]
    return blocks + [(Q_HEADS + 2 * pair, Q_HEADS + 2 * pair + 1) for pair in range(KV_HEADS // 2)]


def _in_proj_attn_kernel(x_ref, g_ref, w_ref, gain_ref, cos_ref, sin_ref, seg_ref, perm_ref, z_ref, h_ref, wp_ref,
                         *, rope_cols):
    @pl.when(pl.program_id(0) == 0)
    def _():
        for blk, heads in enumerate(_pair_block_heads()):
            scattered = [_dot(w_ref[:, hd * HEAD_DIM:(hd + 1) * HEAD_DIM], perm_ref[m]) for m, hd in enumerate(heads)]
            wp_ref[:, blk * 2 * HEAD_DIM:(blk + 1) * 2 * HEAD_DIM] = (scattered[0] + scattered[1]).astype(BF16)

    def project(h, rows, c0):
        w_cols = wp_ref if c0 < rope_cols else w_ref
        acc = _dot(h, w_cols[:, c0:c0 + IN_PROJ_DOT_COLS])
        if c0 >= rope_cols:
            z_ref[rows, c0:c0 + IN_PROJ_DOT_COLS] = acc.astype(BF16)
            return
        cos, sin = cos_ref[rows, :], sin_ref[rows, :]
        for p0 in range(0, IN_PROJ_DOT_COLS, 2 * HEAD_DIM):
            t0, t1 = acc[:, p0:p0 + HEAD_DIM], acc[:, p0 + HEAD_DIM:p0 + 2 * HEAD_DIM]
            lanes0 = slice(c0 + p0, c0 + p0 + HEAD_DIM)
            lanes1 = slice(c0 + p0 + HEAD_DIM, c0 + p0 + 2 * HEAD_DIM)
            ss = t0 * t0 + t1 * t1
            hi = ss.astype(BF16)
            lo = (ss - hi.astype(F32)).astype(BF16)
            tot = _dot(jnp.concatenate([hi, lo], axis=1), seg_ref[...])
            r = lax.rsqrt(tot * (1.0 / HEAD_DIM) + EPS)
            n0 = t0 * r * gain_ref[:, lanes0]
            n1 = t1 * r * gain_ref[:, lanes1]
            z_ref[rows, lanes0] = (n0 * cos - n1 * sin).astype(BF16)
            z_ref[rows, lanes1] = (n1 * cos + n0 * sin).astype(BF16)

    _project_column_chunks(x_ref, g_ref, h_ref, z_ref.shape[1], project)


def _in_proj(x2d, g, w, layer, *, attn=None, casts=()):
    rows, _ = x2d.shape
    cols = w.shape[-1]
    bm = IN_PROJ_ROWS
    assert cols % IN_PROJ_DOT_COLS == 0
    resident = pl.Buffered(1)
    if layer is None:
        w_spec = pl.BlockSpec((D_MODEL, cols), lambda i: (0, 0), pipeline_mode=resident)
    else:
        w_spec = pl.BlockSpec((None, D_MODEL, cols), lambda i: (layer, 0, 0), pipeline_mode=resident)
    in_specs = [
        pl.BlockSpec((bm, D_MODEL), lambda i: (i, 0)),
        pl.BlockSpec((1, D_MODEL), lambda i: (0, 0)),
        w_spec,
    ]
    args = [x2d, g.reshape(1, D_MODEL), w]
    scratch_shapes = [pltpu.VMEM((bm, D_MODEL), BF16)]
    if attn is None:
        assert (2 * TOK_WIDTH) % IN_PROJ_DOT_COLS == 0
        body = functools.partial(_in_proj_gmlp_kernel, gelu_cols=2 * TOK_WIDTH)
    else:
        gain, cos, sin, seg, perm = attn
        seq_tiles = cos.shape[0] // bm
        rope_cols = TOK_WIDTH + KV_WIDTH
        assert rope_cols % IN_PROJ_DOT_COLS == 0
        body = functools.partial(_in_proj_attn_kernel, rope_cols=rope_cols)
        in_specs += [
            pl.BlockSpec((1, cols), lambda i: (0, 0)),
            pl.BlockSpec((bm, HEAD_DIM), lambda i: (i % seq_tiles, 0)),
            pl.BlockSpec((bm, HEAD_DIM), lambda i: (i % seq_tiles, 0)),
            pl.BlockSpec((2 * HEAD_DIM, HEAD_DIM), lambda i: (0, 0)),
            pl.BlockSpec((2, HEAD_DIM, 2 * HEAD_DIM), lambda i: (0, 0, 0)),
        ]
        args += [gain, cos, sin, seg, perm]
        scratch_shapes.append(pltpu.VMEM((D_MODEL, rope_cols), BF16))
    cast_specs = [_weight_cast_specs(*cast, rows // bm, lambda i: i) for cast in casts]
    return pl.pallas_call(
        _with_weight_casts(body, len(in_specs), 1, len(casts)),
        grid=(rows // bm,),
        in_specs=in_specs + [src for src, _, _ in cast_specs],
        out_specs=[pl.BlockSpec((bm, cols), lambda i: (i, 0))] + [dst for _, dst, _ in cast_specs],
        out_shape=[jax.ShapeDtypeStruct((rows, cols), BF16)] + [shape for _, _, shape in cast_specs],
        scratch_shapes=scratch_shapes,
        compiler_params=_params("arbitrary"),
        name="in_proj_gmlp" if attn is None else "in_proj_attn",
    )(*args, *[cast[0] for cast in casts])


_NT_DIMS = (((1,), (1,)), ((), ()))


def _attention_kernel(q_ref, k_ref, v_ref, o_ref, v1_ref):
    @pl.when(pl.program_id(2) == 0)
    def _():
        ones_col = lax.broadcasted_iota(jnp.int32, (v_ref.shape[0], HEAD_DIM), 1) == 0
        v1_ref[:, :HEAD_DIM] = v_ref[...]
        v1_ref[:, HEAD_DIM:] = jnp.where(ones_col, 1.0, 0.0).astype(BF16)

    k = k_ref[...]
    v = v1_ref[...]
    half = HEAD_DIM // 2
    lane = lax.broadcasted_iota(jnp.int32, (ATTN_CHAIN_ROWS, 2 * HEAD_DIM), 1) % HEAD_DIM
    first_lane = (pl.program_id(1) % 2) * half
    keep = jnp.logical_and(lane >= first_lane, lane < first_lane + half)
    for r in range(0, q_ref.shape[0], ATTN_CHAIN_ROWS):
        rows = slice(r, r + ATTN_CHAIN_ROWS)
        for g in range(Q_PER_KV):
            q = q_ref[rows, g * 2 * HEAD_DIM:(g + 1) * 2 * HEAD_DIM]
            q = jnp.where(keep, q, jnp.zeros_like(q))
            s = lax.dot_general(q, k, _NT_DIMS, preferred_element_type=F32)
            p = jnp.exp2((s - jnp.max(s, axis=-1, keepdims=True)).astype(BF16))
            o = _dot(p, v)
            o = o[:, :HEAD_DIM] / o[:, HEAD_DIM:HEAD_DIM + 1]
            o_ref[rows, g * HEAD_DIM:(g + 1) * HEAD_DIM] = o.astype(BF16)


def _attention(z, batch, seq, casts=()):
    bq = ATTN_Q_ROWS
    q_tiles = seq // bq
    pair_width = 2 * HEAD_DIM
    k_block0 = TOK_WIDTH // pair_width
    v_block0 = (TOK_WIDTH + KV_WIDTH) // HEAD_DIM
    grid = (batch, KV_HEADS, q_tiles)
    cast_specs = [_weight_cast_specs(*cast, batch * KV_HEADS * q_tiles,
                                     lambda b, h, i: (b * KV_HEADS + h) * q_tiles + i) for cast in casts]
    in_specs = [
        pl.BlockSpec((bq, Q_PER_KV * pair_width), lambda b, h, i: (b * q_tiles + i, h // 2)),
        pl.BlockSpec((seq, pair_width), lambda b, h, i: (b, k_block0 + h // 2)),
        pl.BlockSpec((seq, HEAD_DIM), lambda b, h, i: (b, v_block0 + h)),
    ]
    return pl.pallas_call(
        _with_weight_casts(_attention_kernel, len(in_specs), 1, len(casts)),
        grid=grid,
        in_specs=in_specs + [src for src, _, _ in cast_specs],
        out_specs=[pl.BlockSpec((bq, Q_PER_KV * HEAD_DIM), lambda b, h, i: (b * q_tiles + i, h))]
        + [dst for _, dst, _ in cast_specs],
        out_shape=[jax.ShapeDtypeStruct((batch * seq, TOK_WIDTH), BF16)] + [shape for _, _, shape in cast_specs],
        scratch_shapes=[pltpu.VMEM((seq, 2 * HEAD_DIM), BF16)],
        compiler_params=_params("parallel", "parallel", "arbitrary"),
        name="attention",
    )(z, z, z, *[cast[0] for cast in casts])


def _memory_attention_into(mixed_ref, qm_ref, g_mq_ref, kt_ref, v_ref):
    for hd in range(MEM_HEADS):
        lanes = slice(hd * HEAD_DIM, (hd + 1) * HEAD_DIM)
        q = qm_ref[:, lanes].astype(F32)
        q = (q * _rms_scale(q) * g_mq_ref[...] * SCORE_SCALE).astype(BF16)
        s = _dot(q, kt_ref[lanes, :])
        p = jnp.exp(s - jnp.max(s, axis=-1, keepdims=True))
        denom = jnp.sum(p, axis=-1, keepdims=True)
        o = _dot(p.astype(BF16), v_ref[:, lanes]) / denom
        mixed_ref[:, TOK_WIDTH + hd * HEAD_DIM:TOK_WIDTH + (hd + 1) * HEAD_DIM] = o.astype(BF16)


def _mix_out_gmlp_kernel(u_ref, v_ref, qm_ref, g_v_ref, ws_ref, bs_ref, g_mq_ref, kt_ref, mv_ref,
                         w_out_ref, x_ref, o_ref, mixed_ref):
    v = v_ref[...].astype(F32)
    vn = (v * _rms_scale(v) * g_v_ref[...]).astype(BF16)
    for c in range(v.shape[0] // CHUNK):
        rows = slice(c * CHUNK, (c + 1) * CHUNK)
        for g in range(A_GROUPS):
            lanes = slice(g * HEAD_DIM, (g + 1) * HEAD_DIM)
            s = _dot(ws_ref[g], vn[rows, lanes]) + bs_ref[g]
            mixed_ref[rows, lanes] = (u_ref[rows, lanes].astype(F32) * s).astype(BF16)
    _memory_attention_into(mixed_ref, qm_ref, g_mq_ref, kt_ref, mv_ref)
    o_ref[...] = x_ref[...] + _dot(mixed_ref[...], w_out_ref[...])


def _mix_out_attn_kernel(t_ref, qm_ref, g_mq_ref, kt_ref, mv_ref, w_out_ref, x_ref, o_ref, mixed_ref):
    mixed_ref[:, :TOK_WIDTH] = t_ref[...]
    _memory_attention_into(mixed_ref, qm_ref, g_mq_ref, kt_ref, mv_ref)
    o_ref[...] = x_ref[...] + _dot(mixed_ref[...], w_out_ref[...])


def _mix_out(x2d, z, mem_kt, mem_v, g_mq, w_out, layer, seq, *, gmlp=None, tok=None, casts=()):
    rows = x2d.shape[0]
    bm = MIX_ROWS_GMLP if gmlp is not None else MIX_ROWS_ATTN
    seq_tiles = seq // bm
    qm_col = (z.shape[1] - MEM_WIDTH) // MEM_WIDTH
    shared_specs = [
        pl.BlockSpec((bm, MEM_WIDTH), lambda i: (i, qm_col)),
    ]
    tail_specs = [
        pl.BlockSpec((1, HEAD_DIM), lambda i: (0, 0)),
        pl.BlockSpec((None, None, MEM_WIDTH, N_MEM), lambda i: (layer, i // seq_tiles, 0, 0)),
        pl.BlockSpec((None, None, N_MEM, MEM_WIDTH), lambda i: (layer, i // seq_tiles, 0, 0)),
        pl.BlockSpec((D_MODEL, D_MODEL), lambda i: (0, 0), pipeline_mode=pl.Buffered(1)),
        pl.BlockSpec((bm, D_MODEL), lambda i: (i, 0)),
    ]
    tail_args = [g_mq.reshape(1, HEAD_DIM), mem_kt, mem_v, w_out, x2d]
    if gmlp is not None:
        g_v, w_s, b_s, gmlp_layer = gmlp
        body = _mix_out_gmlp_kernel
        in_specs = [
            pl.BlockSpec((bm, TOK_WIDTH), lambda i: (i, 0)),
            pl.BlockSpec((bm, TOK_WIDTH), lambda i: (i, 1)),
        ] + shared_specs + [
            pl.BlockSpec((1, TOK_WIDTH), lambda i: (0, 0)),
            pl.BlockSpec((None, A_GROUPS, CHUNK, CHUNK), lambda i: (gmlp_layer, 0, 0, 0)),
            pl.BlockSpec((A_GROUPS, CHUNK, 1), lambda i: (0, 0, 0)),
        ] + tail_specs
        args = [z, z, z, g_v.reshape(1, TOK_WIDTH), w_s, b_s.reshape(A_GROUPS, CHUNK, 1)] + tail_args
        name = "mix_out_gmlp"
    else:
        body = _mix_out_attn_kernel
        in_specs = [pl.BlockSpec((bm, TOK_WIDTH), lambda i: (i, 0))] + shared_specs + tail_specs
        args = [tok, z] + tail_args
        name = "mix_out_attn"
    cast_specs = [_weight_cast_specs(*cast, rows // bm, lambda i: i) for cast in casts]
    return pl.pallas_call(
        _with_weight_casts(body, len(in_specs), 1, len(casts)),
        grid=(rows // bm,),
        in_specs=in_specs + [src for src, _, _ in cast_specs],
        out_specs=[pl.BlockSpec((bm, D_MODEL), lambda i: (i, 0))] + [dst for _, dst, _ in cast_specs],
        out_shape=[jax.ShapeDtypeStruct((rows, D_MODEL), F32)] + [shape for _, _, shape in cast_specs],
        scratch_shapes=[pltpu.VMEM((bm, D_MODEL), BF16)],
        compiler_params=_params("parallel"),
        name=name,
    )(*args, *[cast[0] for cast in casts])


def _ffn_kernel(x_ref, g_ref, wg_ref, wu_ref, wd_ref, o_ref, h_ref):
    j = pl.program_id(1)

    def contribution(h):
        gate = _dot(h, wg_ref[...])
        up = _dot(h, wu_ref[...])
        act = (jax.nn.silu(gate) * up).astype(BF16)
        return _dot(act, wd_ref[...])

    @pl.when(j == 0)
    def _():
        for rows, h in _norm_row_chunks(x_ref, g_ref, h_ref):
            o_ref[rows, :] = x_ref[rows, :] + contribution(h)

    @pl.when(j > 0)
    def _():
        o_ref[...] += contribution(h_ref[...])


def _ffn(x2d, g, w_gate_up, w_down):
    rows = x2d.shape[0]
    d_ff = w_down.shape[0]
    bm, bf = FFN_ROWS, FFN_COLS
    ff_tiles = d_ff // bf
    return pl.pallas_call(
        _ffn_kernel,
        grid=(rows // bm, ff_tiles),
        in_specs=[
            pl.BlockSpec((bm, D_MODEL), lambda i, j: (i, 0)),
            pl.BlockSpec((1, D_MODEL), lambda i, j: (0, 0)),
            pl.BlockSpec((D_MODEL, bf), lambda i, j: (0, j)),
            pl.BlockSpec((D_MODEL, bf), lambda i, j: (0, ff_tiles + j)),
            pl.BlockSpec((bf, D_MODEL), lambda i, j: (j, 0)),
        ],
        out_specs=pl.BlockSpec((bm, D_MODEL), lambda i, j: (i, 0)),
        out_shape=jax.ShapeDtypeStruct((rows, D_MODEL), F32),
        scratch_shapes=[pltpu.VMEM((bm, D_MODEL), BF16)],
        compiler_params=_params("parallel", "arbitrary"),
        name="ffn",
    )(x2d, g.reshape(1, D_MODEL), w_gate_up, w_gate_up, w_down)


def _pair_layout_columns(a):
    lead = a.shape[:-1]
    n = len(lead)
    axes = tuple(range(n))
    half_pairs = KV_HEADS // 2
    q = a[..., :TOK_WIDTH].reshape(lead + (half_pairs, 2, Q_PER_KV, HEAD_DIM))
    q = q.transpose(axes + (n, n + 2, n + 1, n + 3)).reshape(lead + (half_pairs * Q_PER_KV, 2, HEAD_DIM))
    k = a[..., TOK_WIDTH:].reshape(lead + (half_pairs, 2, HEAD_DIM))
    blocks = jnp.concatenate([q, k], axis=n)
    blocks = blocks.reshape(lead + (blocks.shape[n], 2, 2, 2, ROPE_PAIRS))
    blocks = blocks.transpose(axes + (n, n + 3, n + 1, n + 2, n + 4))
    return blocks.reshape(lead + (TOK_WIDTH + KV_WIDTH,))


def _attn_tables(seq):
    n_rows = seq // GRID_W
    rows = jnp.broadcast_to(jnp.arange(n_rows)[:, None], (n_rows, GRID_W)).reshape(seq)
    cols = jnp.broadcast_to(jnp.arange(GRID_W)[None, :], (n_rows, GRID_W)).reshape(seq)
    freqs = ROPE_THETA ** (-jnp.arange(ROPE_PAIRS, dtype=F32) / ROPE_PAIRS)
    ang_r = rows.astype(F32)[:, None] * freqs
    ang_c = cols.astype(F32)[:, None] * freqs
    ang = jnp.concatenate([ang_r, ang_c, ang_r, ang_c], axis=-1)
    k_seg = (jnp.arange(2 * HEAD_DIM) % HEAD_DIM) // (HEAD_DIM // 2)
    n_seg = jnp.arange(HEAD_DIM) // (HEAD_DIM // 2)
    seg = (k_seg[:, None] == n_seg[None, :]).astype(BF16)
    d = jnp.arange(HEAD_DIM)
    axis, half, pair_index = d // (2 * ROPE_PAIRS), (d // ROPE_PAIRS) % 2, d % ROPE_PAIRS
    target = half[None, :] * HEAD_DIM + jnp.arange(2)[:, None] * (HEAD_DIM // 2) + axis[None, :] * ROPE_PAIRS \
        + pair_index[None, :]
    perm = (target[:, :, None] == jnp.arange(2 * HEAD_DIM)[None, None, :]).astype(BF16)
    return jnp.cos(ang), jnp.sin(ang), seg, perm


def _attn_gains(g_q, g_k, cols):
    gains = jnp.concatenate([jnp.tile(g_q * (SCORE_SCALE * LOG2_E), Q_HEADS), jnp.tile(g_k, KV_HEADS)])
    return jnp.concatenate([_pair_layout_columns(gains),
                            jnp.ones((cols - TOK_WIDTH - KV_WIDTH,), F32)]).reshape(1, cols)


def kernel(x, mem, g_mix, g_ffn, w_in_a, g_v_a, w_spatial, b_spatial, w_in_b, g_q_b, g_k_b, g_mem, w_mem_kv,
           g_mq, g_mk, w_out, w_gate_up, w_down):
    batch, seq, _ = x.shape
    rows = batch * seq
    assert seq % IN_PROJ_ROWS == 0 and seq % ATTN_Q_ROWS == 0 and rows % FFN_ROWS == 0
    assert seq % MIX_ROWS_GMLP == 0 and seq % MIX_ROWS_ATTN == 0 and MIX_ROWS_GMLP % CHUNK == 0
    assert w_down.shape[1] % FFN_COLS == 0

    cos, sin, seg, perm = _attn_tables(seq)
    mem_kt, mem_v = _mem_kv(mem, g_mem, w_mem_kv, g_mk)
    w_in_a, w_spatial = w_in_a.astype(BF16), w_spatial.astype(BF16)
    xs = x.reshape(rows, D_MODEL)
    for l in range(DEPTH):
        idx = l // 2
        ffn_casts = ((w_gate_up, l), (w_down, l))
        if l % 2 == 0:
            z, w_out_l, w_out_next, w_in_next = _in_proj(xs, g_mix[l], w_in_a, idx,
                                                         casts=((w_out, l), (w_out, l + 1), (w_in_b, idx)))
            xs, ffn_gate_up, ffn_down = _mix_out(xs, z, mem_kt, mem_v, g_mq[l], w_out_l, l, seq,
                                                 gmlp=(g_v_a[idx], w_spatial, b_spatial[idx], idx),
                                                 casts=ffn_casts)
        else:
            gain = _attn_gains(g_q_b[idx], g_k_b[idx], w_in_next.shape[1])
            z, = _in_proj(xs, g_mix[l], w_in_next, None, attn=(gain, cos, sin, seg, perm))
            tok, ffn_gate_up, ffn_down = _attention(z, batch, seq, casts=ffn_casts)
            xs, = _mix_out(xs, z, mem_kt, mem_v, g_mq[l], w_out_next, l, seq, tok=tok)
        xs = _ffn(xs, g_ffn[l], ffn_gate_up, ffn_down)
    return xs.reshape(batch, seq, D_MODEL)
```

```python
import functools

import jax
import jax.numpy as jnp
from jax import lax
from jax.experimental import pallas as pl
from jax.experimental.pallas import tpu as pltpu

D_MODEL = 2048
DEPTH = 4
N_MEM = 256
GRID_W = 64
HEAD_DIM = 128
MEM_HEADS = 4
MEM_WIDTH = MEM_HEADS * HEAD_DIM
TOK_WIDTH = D_MODEL - MEM_WIDTH
CHUNK = 128
A_GROUPS = TOK_WIDTH // HEAD_DIM
Q_HEADS = TOK_WIDTH // HEAD_DIM
KV_HEADS = 4
Q_PER_KV = Q_HEADS // KV_HEADS
KV_WIDTH = KV_HEADS * HEAD_DIM
ROPE_THETA = 10000.0
ROPE_PAIRS = HEAD_DIM // 4
EPS = 1e-6
SCORE_SCALE = HEAD_DIM ** -0.5

VMEM_LIMIT_BYTES = 63 * 1024 * 1024
MXU_COLS = 256
LOG2_E = 1.4426950408889634

IN_PROJ_ROWS = 1024
IN_PROJ_DOT_COLS = 2 * MXU_COLS
MIX_ROWS_GMLP = 512
MIX_ROWS_ATTN = 1024
ATTN_Q_ROWS = 2048
ATTN_CHAIN_ROWS = 256
FFN_ROWS = 1024
FFN_COLS = 512
NORM_CHUNK_ROWS = 256

BF16 = jnp.bfloat16
F32 = jnp.float32


def _rms_scale(x):
    return lax.rsqrt(jnp.mean(x * x, axis=-1, keepdims=True) + EPS)


def _gelu_exact(x):
    return 0.5 * x * (1.0 + lax.erf(x * (2.0 ** -0.5)))


def _dot(a, b):
    return jnp.dot(a, b, preferred_element_type=F32)


def _params(*semantics):
    return pltpu.CompilerParams(dimension_semantics=semantics, vmem_limit_bytes=VMEM_LIMIT_BYTES)


BF16_TILE_ROWS = 16
LANES = 128


def _weight_cast_specs(w, layer, n_steps, step_id):
    _, rows, cols = w.shape
    n_col = 1 if rows % (BF16_TILE_ROWS * n_steps) == 0 else 2
    n_row = n_steps // n_col
    assert n_row * n_col == n_steps and rows % (BF16_TILE_ROWS * n_row) == 0 and cols % (LANES * n_col) == 0
    block = (rows // n_row, cols // n_col)
    src = pl.BlockSpec((None,) + block, lambda *g: (layer, step_id(*g) // n_col, step_id(*g) % n_col))
    dst = pl.BlockSpec(block, lambda *g: (step_id(*g) // n_col, step_id(*g) % n_col))
    return src, dst, jax.ShapeDtypeStruct((rows, cols), BF16)


def _with_weight_casts(body, n_in, n_out, n_cast):
    def kernel(*refs):
        ins, refs = refs[:n_in], refs[n_in:]
        srcs, refs = refs[:n_cast], refs[n_cast:]
        outs, refs = refs[:n_out], refs[n_out:]
        dsts, scratch = refs[:n_cast], refs[n_cast:]
        for src, dst in zip(srcs, dsts):
            dst[...] = src[...].astype(BF16)
        body(*ins, *outs, *scratch)
    return kernel


def _mem_kv_kernel(mem_ref, g_mem_ref, w_ref, g_mk_ref, kt_ref, v_ref):
    m = mem_ref[...]
    h = (m * _rms_scale(m) * g_mem_ref[...]).astype(BF16)
    kv = _dot(h, w_ref[...].astype(BF16))
    for hd in range(MEM_HEADS):
        k = kv[:, hd * HEAD_DIM:(hd + 1) * HEAD_DIM]
        k = k * _rms_scale(k) * g_mk_ref[...]
        kt_ref[hd * HEAD_DIM:(hd + 1) * HEAD_DIM, :] = k.T.astype(BF16)
    v_ref[...] = kv[:, MEM_WIDTH:].astype(BF16)


def _mem_kv(mem, g_mem, w_mem_kv, g_mk):
    batch = mem.shape[0]
    return pl.pallas_call(
        _mem_kv_kernel,
        grid=(DEPTH, batch),
        in_specs=[
            pl.BlockSpec((None, N_MEM, D_MODEL), lambda l, b: (b, 0, 0)),
            pl.BlockSpec((None, 1, D_MODEL), lambda l, b: (l, 0, 0)),
            pl.BlockSpec((None, D_MODEL, 2 * MEM_WIDTH), lambda l, b: (l, 0, 0)),
            pl.BlockSpec((None, 1, HEAD_DIM), lambda l, b: (l, 0, 0)),
        ],
        out_specs=[
            pl.BlockSpec((None, None, MEM_WIDTH, N_MEM), lambda l, b: (l, b, 0, 0)),
            pl.BlockSpec((None, None, N_MEM, MEM_WIDTH), lambda l, b: (l, b, 0, 0)),
        ],
        out_shape=[
            jax.ShapeDtypeStruct((DEPTH, batch, MEM_WIDTH, N_MEM), BF16),
            jax.ShapeDtypeStruct((DEPTH, batch, N_MEM, MEM_WIDTH), BF16),
        ],
        compiler_params=_params("arbitrary", "arbitrary"),
        name="mem_kv",
    )(mem, g_mem.reshape(DEPTH, 1, D_MODEL), w_mem_kv, g_mk.reshape(DEPTH, 1, HEAD_DIM))


def _norm_row_chunks(x_ref, g_ref, h_ref):
    for r in range(0, x_ref.shape[0], NORM_CHUNK_ROWS):
        rows = slice(r, r + NORM_CHUNK_ROWS)
        x = x_ref[rows, :]
        h = (x * _rms_scale(x) * g_ref[...]).astype(BF16)
        h_ref[rows, :] = h
        yield rows, h


def _project_column_chunks(x_ref, g_ref, h_ref, n_cols, project):
    for rows, h in _norm_row_chunks(x_ref, g_ref, h_ref):
        project(h, rows, 0)
    h = h_ref[...]
    for c0 in range(IN_PROJ_DOT_COLS, n_cols, IN_PROJ_DOT_COLS):
        project(h, slice(0, x_ref.shape[0]), c0)


def _in_proj_gmlp_kernel(x_ref, g_ref, w_ref, z_ref, h_ref, *, gelu_cols):
    def project(h, rows, c0):
        cols = slice(c0, c0 + IN_PROJ_DOT_COLS)
        acc = _dot(h, w_ref[:, cols])
        z_ref[rows, cols] = (_gelu_exact(acc) if c0 < gelu_cols else acc).astype(BF16)

    _project_column_chunks(x_ref, g_ref, h_ref, z_ref.shape[1], project)


def _pair_block_heads():
    blocks = [((2 * pair) * Q_PER_KV + g, (2 * pair + 1) * Q_PER_KV + g)
              for pair in range(KV_HEADS // 2) for g in range(Q_PER_KV)]
    return blocks + [(Q_HEADS + 2 * pair, Q_HEADS + 2 * pair + 1) for pair in range(KV_HEADS // 2)]


def _in_proj_attn_kernel(x_ref, g_ref, w_ref, gain_ref, cos_ref, sin_ref, seg_ref, perm_ref, z_ref, h_ref, wp_ref,
                         *, rope_cols):
    @pl.when(pl.program_id(0) == 0)
    def _():
        for blk, heads in enumerate(_pair_block_heads()):
            scattered = [_dot(w_ref[:, hd * HEAD_DIM:(hd + 1) * HEAD_DIM], perm_ref[m]) for m, hd in enumerate(heads)]
            wp_ref[:, blk * 2 * HEAD_DIM:(blk + 1) * 2 * HEAD_DIM] = (scattered[0] + scattered[1]).astype(BF16)

    def project(h, rows, c0):
        w_cols = wp_ref if c0 < rope_cols else w_ref
        acc = _dot(h, w_cols[:, c0:c0 + IN_PROJ_DOT_COLS])
        if c0 >= rope_cols:
            z_ref[rows, c0:c0 + IN_PROJ_DOT_COLS] = acc.astype(BF16)
            return
        cos, sin = cos_ref[rows, :], sin_ref[rows, :]
        for p0 in range(0, IN_PROJ_DOT_COLS, 2 * HEAD_DIM):
            t0, t1 = acc[:, p0:p0 + HEAD_DIM], acc[:, p0 + HEAD_DIM:p0 + 2 * HEAD_DIM]
            lanes0 = slice(c0 + p0, c0 + p0 + HEAD_DIM)
            lanes1 = slice(c0 + p0 + HEAD_DIM, c0 + p0 + 2 * HEAD_DIM)
            ss = t0 * t0 + t1 * t1
            hi = ss.astype(BF16)
            lo = (ss - hi.astype(F32)).astype(BF16)
            tot = _dot(jnp.concatenate([hi, lo], axis=1), seg_ref[...])
            r = lax.rsqrt(tot * (1.0 / HEAD_DIM) + EPS)
            n0 = t0 * r * gain_ref[:, lanes0]
            n1 = t1 * r * gain_ref[:, lanes1]
            z_ref[rows, lanes0] = (n0 * cos - n1 * sin).astype(BF16)
            z_ref[rows, lanes1] = (n1 * cos + n0 * sin).astype(BF16)

    _project_column_chunks(x_ref, g_ref, h_ref, z_ref.shape[1], project)


def _in_proj(x2d, g, w, layer, *, attn=None, casts=()):
    rows, _ = x2d.shape
    cols = w.shape[-1]
    bm = IN_PROJ_ROWS
    assert cols % IN_PROJ_DOT_COLS == 0
    resident = pl.Buffered(1)
    if layer is None:
        w_spec = pl.BlockSpec((D_MODEL, cols), lambda i: (0, 0), pipeline_mode=resident)
    else:
        w_spec = pl.BlockSpec((None, D_MODEL, cols), lambda i: (layer, 0, 0), pipeline_mode=resident)
    in_specs = [
        pl.BlockSpec((bm, D_MODEL), lambda i: (i, 0)),
        pl.BlockSpec((1, D_MODEL), lambda i: (0, 0)),
        w_spec,
    ]
    args = [x2d, g.reshape(1, D_MODEL), w]
    scratch_shapes = [pltpu.VMEM((bm, D_MODEL), BF16)]
    if attn is None:
        assert (2 * TOK_WIDTH) % IN_PROJ_DOT_COLS == 0
        body = functools.partial(_in_proj_gmlp_kernel, gelu_cols=2 * TOK_WIDTH)
    else:
        gain, cos, sin, seg, perm = attn
        seq_tiles = cos.shape[0] // bm
        rope_cols = TOK_WIDTH + KV_WIDTH
        assert rope_cols % IN_PROJ_DOT_COLS == 0
        body = functools.partial(_in_proj_attn_kernel, rope_cols=rope_cols)
        in_specs += [
            pl.BlockSpec((1, cols), lambda i: (0, 0)),
            pl.BlockSpec((bm, HEAD_DIM), lambda i: (i % seq_tiles, 0)),
            pl.BlockSpec((bm, HEAD_DIM), lambda i: (i % seq_tiles, 0)),
            pl.BlockSpec((2 * HEAD_DIM, HEAD_DIM), lambda i: (0, 0)),
            pl.BlockSpec((2, HEAD_DIM, 2 * HEAD_DIM), lambda i: (0, 0, 0)),
        ]
        args += [gain, cos, sin, seg, perm]
        scratch_shapes.append(pltpu.VMEM((D_MODEL, rope_cols), BF16))
    cast_specs = [_weight_cast_specs(*cast, rows // bm, lambda i: i) for cast in casts]
    return pl.pallas_call(
        _with_weight_casts(body, len(in_specs), 1, len(casts)),
        grid=(rows // bm,),
        in_specs=in_specs + [src for src, _, _ in cast_specs],
        out_specs=[pl.BlockSpec((bm, cols), lambda i: (i, 0))] + [dst for _, dst, _ in cast_specs],
        out_shape=[jax.ShapeDtypeStruct((rows, cols), BF16)] + [shape for _, _, shape in cast_specs],
        scratch_shapes=scratch_shapes,
        compiler_params=_params("arbitrary"),
        name="in_proj_gmlp" if attn is None else "in_proj_attn",
    )(*args, *[cast[0] for cast in casts])


_NT_DIMS = (((1,), (1,)), ((), ()))


def _attention_kernel(q_ref, k_ref, v_ref, o_ref, v1_ref):
    @pl.when(pl.program_id(2) == 0)
    def _():
        ones_col = lax.broadcasted_iota(jnp.int32, (v_ref.shape[0], HEAD_DIM), 1) == 0
        v1_ref[:, :HEAD_DIM] = v_ref[...]
        v1_ref[:, HEAD_DIM:] = jnp.where(ones_col, 1.0, 0.0).astype(BF16)

    k = k_ref[...]
    v = v1_ref[...]
    half = HEAD_DIM // 2
    lane = lax.broadcasted_iota(jnp.int32, (ATTN_CHAIN_ROWS, 2 * HEAD_DIM), 1) % HEAD_DIM
    first_lane = (pl.program_id(1) % 2) * half
    keep = jnp.logical_and(lane >= first_lane, lane < first_lane + half)
    for r in range(0, q_ref.shape[0], ATTN_CHAIN_ROWS):
        rows = slice(r, r + ATTN_CHAIN_ROWS)
        for g in range(Q_PER_KV):
            q = q_ref[rows, g * 2 * HEAD_DIM:(g + 1) * 2 * HEAD_DIM]
            q = jnp.where(keep, q, jnp.zeros_like(q))
            s = lax.dot_general(q, k, _NT_DIMS, preferred_element_type=F32)
            p = jnp.exp2((s - jnp.max(s, axis=-1, keepdims=True)).astype(BF16))
            o = _dot(p, v)
            o = o[:, :HEAD_DIM] / o[:, HEAD_DIM:HEAD_DIM + 1]
            o_ref[rows, g * HEAD_DIM:(g + 1) * HEAD_DIM] = o.astype(BF16)


def _attention(z, batch, seq, casts=()):
    bq = ATTN_Q_ROWS
    q_tiles = seq // bq
    pair_width = 2 * HEAD_DIM
    k_block0 = TOK_WIDTH // pair_width
    v_block0 = (TOK_WIDTH + KV_WIDTH) // HEAD_DIM
    grid = (batch, KV_HEADS, q_tiles)
    cast_specs = [_weight_cast_specs(*cast, batch * KV_HEADS * q_tiles,
                                     lambda b, h, i: (b * KV_HEADS + h) * q_tiles + i) for cast in casts]
    in_specs = [
        pl.BlockSpec((bq, Q_PER_KV * pair_width), lambda b, h, i: (b * q_tiles + i, h // 2)),
        pl.BlockSpec((seq, pair_width), lambda b, h, i: (b, k_block0 + h // 2)),
        pl.BlockSpec((seq, HEAD_DIM), lambda b, h, i: (b, v_block0 + h)),
    ]
    return pl.pallas_call(
        _with_weight_casts(_attention_kernel, len(in_specs), 1, len(casts)),
        grid=grid,
        in_specs=in_specs + [src for src, _, _ in cast_specs],
        out_specs=[pl.BlockSpec((bq, Q_PER_KV * HEAD_DIM), lambda b, h, i: (b * q_tiles + i, h))]
        + [dst for _, dst, _ in cast_specs],
        out_shape=[jax.ShapeDtypeStruct((batch * seq, TOK_WIDTH), BF16)] + [shape for _, _, shape in cast_specs],
        scratch_shapes=[pltpu.VMEM((seq, 2 * HEAD_DIM), BF16)],
        compiler_params=_params("parallel", "parallel", "arbitrary"),
        name="attention",
    )(z, z, z, *[cast[0] for cast in casts])


def _memory_attention_into(mixed_ref, qm_ref, g_mq_ref, kt_ref, v_ref):
    for hd in range(MEM_HEADS):
        lanes = slice(hd * HEAD_DIM, (hd + 1) * HEAD_DIM)
        q = qm_ref[:, lanes].astype(F32)
        q = (q * _rms_scale(q) * g_mq_ref[...] * SCORE_SCALE).astype(BF16)
        s = _dot(q, kt_ref[lanes, :])
        p = jnp.exp(s - jnp.max(s, axis=-1, keepdims=True))
        denom = jnp.sum(p, axis=-1, keepdims=True)
        o = _dot(p.astype(BF16), v_ref[:, lanes]) / denom
        mixed_ref[:, TOK_WIDTH + hd * HEAD_DIM:TOK_WIDTH + (hd + 1) * HEAD_DIM] = o.astype(BF16)


def _mix_out_gmlp_kernel(z_ref, g_v_ref, ws_ref, bs_ref, g_mq_ref, kt_ref, mv_ref,
                         w_out_ref, x_ref, o_ref, mixed_ref):
    u_ref = z_ref.at[:, 0:TOK_WIDTH]
    qm_ref = z_ref.at[:, 2 * TOK_WIDTH:2 * TOK_WIDTH + MEM_WIDTH]
    v = z_ref[:, TOK_WIDTH:2 * TOK_WIDTH].astype(F32)
    vn = (v * _rms_scale(v) * g_v_ref[...]).astype(BF16)
    for c in range(v.shape[0] // CHUNK):
        rows = slice(c * CHUNK, (c + 1) * CHUNK)
        for g in range(A_GROUPS):
            lanes = slice(g * HEAD_DIM, (g + 1) * HEAD_DIM)
            s = _dot(ws_ref[g], vn[rows, lanes]) + bs_ref[g]
            mixed_ref[rows, lanes] = (u_ref[rows, lanes].astype(F32) * s).astype(BF16)
    _memory_attention_into(mixed_ref, qm_ref, g_mq_ref, kt_ref, mv_ref)
    o_ref[...] = x_ref[...] + _dot(mixed_ref[...], w_out_ref[...])


def _mix_out_attn_kernel(t_ref, qm_ref, g_mq_ref, kt_ref, mv_ref, w_out_ref, x_ref, o_ref, mixed_ref):
    mixed_ref[:, :TOK_WIDTH] = t_ref[...]
    _memory_attention_into(mixed_ref, qm_ref, g_mq_ref, kt_ref, mv_ref)
    o_ref[...] = x_ref[...] + _dot(mixed_ref[...], w_out_ref[...])


def _mix_out(x2d, z, mem_kt, mem_v, g_mq, w_out, layer, seq, *, gmlp=None, tok=None, casts=()):
    rows = x2d.shape[0]
    bm = MIX_ROWS_GMLP if gmlp is not None else MIX_ROWS_ATTN
    seq_tiles = seq // bm
    qm_col = (z.shape[1] - MEM_WIDTH) // MEM_WIDTH
    shared_specs = [
        pl.BlockSpec((bm, MEM_WIDTH), lambda i: (i, qm_col)),
    ]
    tail_specs = [
        pl.BlockSpec((1, HEAD_DIM), lambda i: (0, 0)),
        pl.BlockSpec((None, None, MEM_WIDTH, N_MEM), lambda i: (layer, i // seq_tiles, 0, 0)),
        pl.BlockSpec((None, None, N_MEM, MEM_WIDTH), lambda i: (layer, i // seq_tiles, 0, 0)),
        pl.BlockSpec((D_MODEL, D_MODEL), lambda i: (0, 0), pipeline_mode=pl.Buffered(1)),
        pl.BlockSpec((bm, D_MODEL), lambda i: (i, 0)),
    ]
    tail_args = [g_mq.reshape(1, HEAD_DIM), mem_kt, mem_v, w_out, x2d]
    if gmlp is not None:
        g_v, w_s, b_s, gmlp_layer = gmlp
        body = _mix_out_gmlp_kernel
        in_specs = [
            pl.BlockSpec((bm, z.shape[1]), lambda i: (i, 0)),
            pl.BlockSpec((1, TOK_WIDTH), lambda i: (0, 0)),
            pl.BlockSpec((None, A_GROUPS, CHUNK, CHUNK), lambda i: (gmlp_layer, 0, 0, 0)),
            pl.BlockSpec((A_GROUPS, CHUNK, 1), lambda i: (0, 0, 0)),
        ] + tail_specs
        args = [z, g_v.reshape(1, TOK_WIDTH), w_s, b_s.reshape(A_GROUPS, CHUNK, 1)] + tail_args
        name = "mix_out_gmlp"
    else:
        body = _mix_out_attn_kernel
        in_specs = [pl.BlockSpec((bm, TOK_WIDTH), lambda i: (i, 0))] + shared_specs + tail_specs
        args = [tok, z] + tail_args
        name = "mix_out_attn"
    cast_specs = [_weight_cast_specs(*cast, rows // bm, lambda i: i) for cast in casts]
    return pl.pallas_call(
        _with_weight_casts(body, len(in_specs), 1, len(casts)),
        grid=(rows // bm,),
        in_specs=in_specs + [src for src, _, _ in cast_specs],
        out_specs=[pl.BlockSpec((bm, D_MODEL), lambda i: (i, 0))] + [dst for _, dst, _ in cast_specs],
        out_shape=[jax.ShapeDtypeStruct((rows, D_MODEL), F32)] + [shape for _, _, shape in cast_specs],
        scratch_shapes=[pltpu.VMEM((bm, D_MODEL), BF16)],
        compiler_params=_params("parallel"),
        name=name,
    )(*args, *[cast[0] for cast in casts])


def _ffn_kernel(x_ref, g_ref, wg_ref, wu_ref, wd_ref, o_ref, h_ref):
    j = pl.program_id(1)

    def contribution(h):
        gate = _dot(h, wg_ref[...])
        up = _dot(h, wu_ref[...])
        act = (jax.nn.silu(gate) * up).astype(BF16)
        return _dot(act, wd_ref[...])

    @pl.when(j == 0)
    def _():
        for rows, h in _norm_row_chunks(x_ref, g_ref, h_ref):
            o_ref[rows, :] = x_ref[rows, :] + contribution(h)

    @pl.when(j > 0)
    def _():
        o_ref[...] += contribution(h_ref[...])


def _ffn(x2d, g, w_gate_up, w_down):
    rows = x2d.shape[0]
    d_ff = w_down.shape[0]
    bm, bf = FFN_ROWS, FFN_COLS
    ff_tiles = d_ff // bf
    return pl.pallas_call(
        _ffn_kernel,
        grid=(rows // bm, ff_tiles),
        in_specs=[
            pl.BlockSpec((bm, D_MODEL), lambda i, j: (i, 0)),
            pl.BlockSpec((1, D_MODEL), lambda i, j: (0, 0)),
            pl.BlockSpec((D_MODEL, bf), lambda i, j: (0, j)),
            pl.BlockSpec((D_MODEL, bf), lambda i, j: (0, ff_tiles + j)),
            pl.BlockSpec((bf, D_MODEL), lambda i, j: (j, 0)),
        ],
        out_specs=pl.BlockSpec((bm, D_MODEL), lambda i, j: (i, 0)),
        out_shape=jax.ShapeDtypeStruct((rows, D_MODEL), F32),
        scratch_shapes=[pltpu.VMEM((bm, D_MODEL), BF16)],
        compiler_params=_params("parallel", "arbitrary"),
        name="ffn",
    )(x2d, g.reshape(1, D_MODEL), w_gate_up, w_gate_up, w_down)


def _pair_layout_columns(a):
    lead = a.shape[:-1]
    n = len(lead)
    axes = tuple(range(n))
    half_pairs = KV_HEADS // 2
    q = a[..., :TOK_WIDTH].reshape(lead + (half_pairs, 2, Q_PER_KV, HEAD_DIM))
    q = q.transpose(axes + (n, n + 2, n + 1, n + 3)).reshape(lead + (half_pairs * Q_PER_KV, 2, HEAD_DIM))
    k = a[..., TOK_WIDTH:].reshape(lead + (half_pairs, 2, HEAD_DIM))
    blocks = jnp.concatenate([q, k], axis=n)
    blocks = blocks.reshape(lead + (blocks.shape[n], 2, 2, 2, ROPE_PAIRS))
    blocks = blocks.transpose(axes + (n, n + 3, n + 1, n + 2, n + 4))
    return blocks.reshape(lead + (TOK_WIDTH + KV_WIDTH,))


def _attn_tables(seq):
    n_rows = seq // GRID_W
    rows = jnp.broadcast_to(jnp.arange(n_rows)[:, None], (n_rows, GRID_W)).reshape(seq)
    cols = jnp.broadcast_to(jnp.arange(GRID_W)[None, :], (n_rows, GRID_W)).reshape(seq)
    freqs = ROPE_THETA ** (-jnp.arange(ROPE_PAIRS, dtype=F32) / ROPE_PAIRS)
    ang_r = rows.astype(F32)[:, None] * freqs
    ang_c = cols.astype(F32)[:, None] * freqs
    ang = jnp.concatenate([ang_r, ang_c, ang_r, ang_c], axis=-1)
    k_seg = (jnp.arange(2 * HEAD_DIM) % HEAD_DIM) // (HEAD_DIM // 2)
    n_seg = jnp.arange(HEAD_DIM) // (HEAD_DIM // 2)
    seg = (k_seg[:, None] == n_seg[None, :]).astype(BF16)
    d = jnp.arange(HEAD_DIM)
    axis, half, pair_index = d // (2 * ROPE_PAIRS), (d // ROPE_PAIRS) % 2, d % ROPE_PAIRS
    target = half[None, :] * HEAD_DIM + jnp.arange(2)[:, None] * (HEAD_DIM // 2) + axis[None, :] * ROPE_PAIRS \
        + pair_index[None, :]
    perm = (target[:, :, None] == jnp.arange(2 * HEAD_DIM)[None, None, :]).astype(BF16)
    return jnp.cos(ang), jnp.sin(ang), seg, perm


def _attn_gains(g_q, g_k, cols):
    gains = jnp.concatenate([jnp.tile(g_q * (SCORE_SCALE * LOG2_E), Q_HEADS), jnp.tile(g_k, KV_HEADS)])
    return jnp.concatenate([_pair_layout_columns(gains),
                            jnp.ones((cols - TOK_WIDTH - KV_WIDTH,), F32)]).reshape(1, cols)


def kernel(x, mem, g_mix, g_ffn, w_in_a, g_v_a, w_spatial, b_spatial, w_in_b, g_q_b, g_k_b, g_mem, w_mem_kv,
           g_mq, g_mk, w_out, w_gate_up, w_down):
    batch, seq, _ = x.shape
    rows = batch * seq
    assert seq % IN_PROJ_ROWS == 0 and seq % ATTN_Q_ROWS == 0 and rows % FFN_ROWS == 0
    assert seq % MIX_ROWS_GMLP == 0 and seq % MIX_ROWS_ATTN == 0 and MIX_ROWS_GMLP % CHUNK == 0
    assert w_down.shape[1] % FFN_COLS == 0

    cos, sin, seg, perm = _attn_tables(seq)
    mem_kt, mem_v = _mem_kv(mem, g_mem, w_mem_kv, g_mk)
    w_in_a, w_spatial = w_in_a.astype(BF16), w_spatial.astype(BF16)
    xs = x.reshape(rows, D_MODEL)
    for l in range(DEPTH):
        idx = l // 2
        ffn_casts = ((w_gate_up, l), (w_down, l))
        if l % 2 == 0:
            z, w_out_l, w_out_next, w_in_next = _in_proj(xs, g_mix[l], w_in_a, idx,
                                                         casts=((w_out, l), (w_out, l + 1), (w_in_b, idx)))
            xs, ffn_gate_up, ffn_down = _mix_out(xs, z, mem_kt, mem_v, g_mq[l], w_out_l, l, seq,
                                                 gmlp=(g_v_a[idx], w_spatial, b_spatial[idx], idx),
                                                 casts=ffn_casts)
        else:
            gain = _attn_gains(g_q_b[idx], g_k_b[idx], w_in_next.shape[1])
            z, = _in_proj(xs, g_mix[l], w_in_next, None, attn=(gain, cos, sin, seg, perm))
            tok, ffn_gate_up, ffn_down = _attention(z, batch, seq, casts=ffn_casts)
            xs, = _mix_out(xs, z, mem_kt, mem_v, g_mq[l], w_out_next, l, seq, tok=tok)
        xs = _ffn(xs, g_ffn[l], ffn_gate_up, ffn_down)
    return xs.reshape(batch, seq, D_MODEL)
```
